```python
import jax, jax.numpy as jnp
from jax import lax
import numpy as np

D_MODEL = 1024
BATCH = 4
SEQ = 8192
DEPTH = 2

ATTN_WIDTH = D_MODEL // 2
HEAD_DIM = 64
N_ATTN_HEADS = ATTN_WIDTH // HEAD_DIM
CONV_WIDTH = D_MODEL - ATTN_WIDTH
CONV_K = 3
DILATED_BRANCHES = ((128, 1), (512, 4), (2048, 16))
BLOCK = 128
D_FF = 2816
N_SUB = 3
N_MOD = 3
IN_COLS = 3 * ATTN_WIDTH + 3 * CONV_WIDTH
EPS = 1e-6
NEG = -1e30

kernel_name = "hymba_dilated_attn_shortconv_macaron"


def _rmsnorm(x, g):
    xf = x.astype(jnp.float32)
    y = xf * lax.rsqrt(jnp.mean(xf * xf, axis=-1, keepdims=True) + EPS)
    return (y * g.astype(jnp.float32)).astype(x.dtype)


def _band_attention(q, k, v, span):
    n, L, h, hd = q.shape
    nb = L // BLOCK
    qb = q.reshape(n, nb, BLOCK, h, hd)
    kb = k.reshape(n, nb, BLOCK, h, hd)
    vb = v.reshape(n, nb, BLOCK, h, hd)
    zk = jnp.zeros_like(kb[:, :1])
    k2 = jnp.concatenate([jnp.concatenate([zk, kb[:, :-1]], axis=1), kb], axis=2)
    v2 = jnp.concatenate([jnp.concatenate([zk, vb[:, :-1]], axis=1), vb], axis=2)
    s = jnp.einsum('nbqhd,nbkhd->nbhqk', qb, k2).astype(jnp.float32) * (hd ** -0.5)
    qi = jnp.arange(BLOCK)[:, None] + BLOCK
    kj = jnp.arange(2 * BLOCK)[None, :]
    dist = qi - kj
    band = (dist >= 0) & (dist <= span)
    blk = jnp.arange(nb)[:, None, None]
    mask = band[None] & ((blk > 0) | (kj[None] >= BLOCK))
    s = jnp.where(mask[None, :, None], s, NEG)
    m = jnp.max(s, axis=-1, keepdims=True)
    p = jnp.exp(s - m)
    l = jnp.sum(p, axis=-1, keepdims=True)
    o = jnp.einsum('nbhqk,nbkhd->nbqhd', (p / l).astype(v.dtype), v2)
    lse = (m + jnp.log(l))[..., 0]
    return o.reshape(n, L, h, hd), lse.transpose(0, 1, 3, 2).reshape(n, L, h)


def _dilated_branch(q, k, v, window, dilation):
    b, s, h, hd = q.shape
    L = s // dilation
    Lp = -(-L // BLOCK) * BLOCK

    def to_res(t):
        t = t.reshape(b, L, dilation, h, hd).transpose(0, 2, 1, 3, 4).reshape(b * dilation, L, h, hd)
        return jnp.pad(t, ((0, 0), (0, Lp - L), (0, 0), (0, 0)))

    o, lse = _band_attention(to_res(q), to_res(k), to_res(v), window // dilation)
    o = o[:, :L].reshape(b, dilation, L, h, hd).transpose(0, 2, 1, 3, 4).reshape(b, s, h, hd)
    lse = lse[:, :L].reshape(b, dilation, L, h).transpose(0, 2, 1, 3).reshape(b, s, h)
    return o, lse


def _dilated_attention(q, k, v):
    outs, lses = [], []
    for window, dilation in DILATED_BRANCHES:
        o, lse = _dilated_branch(q, k, v, window, dilation)
        outs.append(o)
        lses.append(lse)
    wts = jax.nn.softmax(jnp.stack(lses, axis=-1), axis=-1)
    return jnp.einsum('bshr,rbshd->bshd', wts.astype(q.dtype), jnp.stack(outs, axis=0))


def _short_conv(u, w, bias):
    y = lax.conv_general_dilated(
        u, w[:, None, :].astype(u.dtype), window_strides=(1,),
        padding=((CONV_K - 1, 0),), dimension_numbers=('NWC', 'WIO', 'NWC'),
        feature_group_count=u.shape[-1])
    return y + bias


def _swiglu(h, w1, w2):
    g, up = jnp.split(h @ w1, 2, axis=-1)
    return (jax.nn.silu(g) * up) @ w2


def _mixer(h, w_in, q_g, k_g, conv_w, conv_b, w_out):
    b, s, _ = h.shape
    A, C = ATTN_WIDTH, CONV_WIDTH
    proj = h @ w_in
    q, k, v, gb, gc, u = jnp.split(proj, [A, 2 * A, 3 * A, 3 * A + C, 3 * A + 2 * C], axis=-1)
    q = _rmsnorm(q.reshape(b, s, N_ATTN_HEADS, HEAD_DIM), q_g)
    k = _rmsnorm(k.reshape(b, s, N_ATTN_HEADS, HEAD_DIM), k_g)
    v = v.reshape(b, s, N_ATTN_HEADS, HEAD_DIM)
    y_attn = _dilated_attention(q, k, v).reshape(b, s, A)
    y_conv = gb * _short_conv(gc * u, conv_w, conv_b)
    return jnp.concatenate([y_attn, y_conv], axis=-1) @ w_out


def setup_inputs(seed: int = 0) -> dict:
    key = jax.random.key(seed)
    ks = jax.random.split(key, 14)
    D = D_MODEL
    nrm = jax.random.normal
    return {
        "x": nrm(ks[0], (BATCH, SEQ, D), jnp.float32),
        "c": nrm(ks[1], (BATCH, D), jnp.float32),
        "w_ada": nrm(ks[2], (DEPTH, D, N_SUB * N_MOD * D), jnp.float32) * (0.5 * D ** -0.5),
        "b_ada": nrm(ks[3], (DEPTH, N_SUB * N_MOD * D), jnp.float32) * 0.02,
        "norm_g": 1.0 + 0.02 * nrm(ks[4], (DEPTH, N_SUB, D), jnp.float32),
        "w_in": nrm(ks[5], (DEPTH, D, IN_COLS), jnp.float32) * D ** -0.5,
        "q_norm_g": 1.0 + 0.02 * nrm(ks[6], (DEPTH, HEAD_DIM), jnp.float32),
        "k_norm_g": 1.0 + 0.02 * nrm(ks[7], (DEPTH, HEAD_DIM), jnp.float32),
        "conv_w": nrm(ks[8], (DEPTH, CONV_K, CONV_WIDTH), jnp.float32) * CONV_K ** -0.5,
        "conv_b": nrm(ks[9], (DEPTH, CONV_WIDTH), jnp.float32) * 0.02,
        "w_out": nrm(ks[10], (DEPTH, D, D), jnp.float32) * D ** -0.5,
        "ffn_w1": nrm(ks[11], (DEPTH, 2, D, 2 * D_FF), jnp.float32) * D ** -0.5,
        "ffn_w2": nrm(ks[12], (DEPTH, 2, D_FF, D), jnp.float32) * D_FF ** -0.5,
    }


def reference(x, c, w_ada, b_ada, norm_g, w_in, q_norm_g, k_norm_g, conv_w, conv_b,
              w_out, ffn_w1, ffn_w2):
    b = x.shape[0]
    for layer in range(DEPTH):
        mod = (jax.nn.silu(c) @ w_ada[layer] + b_ada[layer]).reshape(b, N_SUB, N_MOD, D_MODEL)
        shift = mod[:, :, 0, None, :]
        scale = mod[:, :, 1, None, :]
        gate = mod[:, :, 2, None, :]

        def ada(z, i):
            return _rmsnorm(z, norm_g[layer, i]) * (1.0 + scale[:, i]) + shift[:, i]

        x = x + 0.5 * gate[:, 0] * _swiglu(ada(x, 0), ffn_w1[layer, 0], ffn_w2[layer, 0])
        x = x + gate[:, 1] * _mixer(ada(x, 1), w_in[layer], q_norm_g[layer], k_norm_g[layer],
                                    conv_w[layer], conv_b[layer], w_out[layer])
        x = x + 0.5 * gate[:, 2] * _swiglu(ada(x, 2), ffn_w1[layer, 1], ffn_w2[layer, 1])
    return x
```

```python
import functools

import jax
import jax.numpy as jnp
from jax import lax
from jax.experimental import pallas as pl
from jax.experimental.pallas import tpu as pltpu

EPS = 1e-6
NEG = -1e30
HEAD_DIM = 64
N_SUB = 3
N_MOD = 3
CONV_K = 3
DILATIONS = (1, 4, 16)
ATTN_BLOCK = 128
LANES = 128
SUBLANES = 8
ATTN_TILE = ATTN_BLOCK * DILATIONS[-1]
VMEM_LIMIT_BYTES = 56 * 1024 * 1024

_BF16 = jnp.bfloat16
_F32 = jnp.float32


def _params(n_axes):
    return pltpu.CompilerParams(
        dimension_semantics=("arbitrary",) * n_axes,
        vmem_limit_bytes=VMEM_LIMIT_BYTES)


def _resident(shape):
    nd = len(shape)
    return pl.BlockSpec(shape, lambda *_: (0,) * nd, pipeline_mode=pl.Buffered(1))


def _silu(z):
    return z * (1.0 / (1.0 + jnp.exp(-z)))


def _ada_norm(x, g, shift, scale):
    y = x * lax.rsqrt(jnp.mean(x * x, axis=-1, keepdims=True) + EPS)
    return (y * g) * (1.0 + scale) + shift


def _mod_kernel(c_ref, w_ref, b_ref, o_ref):
    a = _silu(c_ref[...]).astype(_BF16)
    o_ref[0] = jnp.dot(a, w_ref[0].astype(_BF16),
                       preferred_element_type=_F32) + b_ref[0]


def _modulation(c_pad, w_ada, b_ada):
    depth, d, n = w_ada.shape
    rows = c_pad.shape[0]
    tn = 1024
    return pl.pallas_call(
        _mod_kernel,
        grid=(depth, n // tn),
        in_specs=[
            pl.BlockSpec((rows, d), lambda l, j: (0, 0)),
            pl.BlockSpec((1, d, tn), lambda l, j: (l, 0, j)),
            pl.BlockSpec((1, 1, tn), lambda l, j: (l, 0, j)),
        ],
        out_specs=pl.BlockSpec((1, rows, tn), lambda l, j: (l, 0, j)),
        out_shape=jax.ShapeDtypeStruct((depth, rows, n), _F32),
        compiler_params=_params(2),
        name="ada_modulation",
    )(c_pad, w_ada, b_ada.reshape(depth, 1, n))


def _ffn_kernel(x_ref, g_ref, shift_ref, scale_ref, gate_ref, w1_ref, w2_ref, o_ref):
    x = x_ref[0]
    h = _ada_norm(x, g_ref[...], shift_ref[0], scale_ref[0]).astype(_BF16)
    gu = jnp.dot(h, w1_ref[...], preferred_element_type=_F32)
    f = w2_ref.shape[0]
    act = (_silu(gu[:, :f]) * gu[:, f:]).astype(_BF16)
    y = jnp.dot(act, w2_ref[...], preferred_element_type=_F32)
    o_ref[0] = x + (0.5 * gate_ref[0]) * y


def _ffn(x, g, shift, scale, gate, w1, w2, tm=512):
    b, s, d = x.shape
    row = pl.BlockSpec((1, tm, d), lambda i, j: (i, j, 0))
    vec = pl.BlockSpec((1, 1, d), lambda i, j: (i, 0, 0))
    return pl.pallas_call(
        _ffn_kernel,
        grid=(b, s // tm),
        in_specs=[row, pl.BlockSpec((1, d), lambda i, j: (0, 0)), vec, vec, vec,
                  _resident(w1.shape), _resident(w2.shape)],
        out_specs=row,
        out_shape=jax.ShapeDtypeStruct(x.shape, x.dtype),
        compiler_params=_params(2),
        name="swiglu_half_step",
    )(x, g, shift, scale, gate, w1, w2)


def _head_rmsnorm(z, gain, pool):
    zz = z * z
    hi = zz.astype(_BF16)
    lo = (zz - hi.astype(_F32)).astype(_BF16)
    ms = (jnp.dot(hi, pool, preferred_element_type=_F32)
          + jnp.dot(lo, pool, preferred_element_type=_F32))
    return (z * lax.rsqrt(ms + EPS)) * gain


def _proj_kernel(x_ref, g_ref, shift_ref, scale_ref, w_in_ref, pool_ref, qg_ref,
                 kg_ref, cw_ref, cb_ref, q_ref, k_ref, v_ref, yc_ref, cv_ref):
    tm = x_ref.shape[1]
    a = pool_ref.shape[0]
    halo = SUBLANES

    @pl.when(pl.program_id(1) == 0)
    def _():
        cv_ref[0:halo, :] = jnp.zeros((halo, a), _F32)

    h = _ada_norm(x_ref[0], g_ref[...], shift_ref[0], scale_ref[0]).astype(_BF16)
    proj = jnp.dot(h, w_in_ref[...], preferred_element_type=_F32)
    pool = pool_ref[...]
    q = _head_rmsnorm(proj[:, 0:a], qg_ref[...], pool) * (HEAD_DIM ** -0.5)
    k = _head_rmsnorm(proj[:, a:2 * a], kg_ref[...], pool)
    v = proj[:, 2 * a:3 * a]
    for hp in range(a // LANES):
        cols = slice(hp * LANES, (hp + 1) * LANES)
        q_ref[0, hp] = q[:, cols].astype(_BF16)
        k_ref[0, hp] = k[:, cols].astype(_BF16)
        v_ref[0, hp] = v[:, cols].astype(_BF16)

    gate_b = proj[:, 3 * a:4 * a]
    cv = proj[:, 4 * a:5 * a] * proj[:, 5 * a:6 * a]
    cv_ref[halo:halo + tm, :] = cv
    conv = (cw_ref[0:1, :] * cv_ref[halo - 2:halo - 2 + tm, :]
            + cw_ref[1:2, :] * cv_ref[halo - 1:halo - 1 + tm, :]
            + cw_ref[2:3, :] * cv)
    yc_ref[0] = (gate_b * (conv + cb_ref[...])).astype(_BF16)
    cv_ref[0:halo, :] = cv_ref[tm:tm + halo, :]


def _mixer_proj(x, g, shift, scale, w_in, pool, qg, kg, conv_w, conv_b, tm=512):
    b, s, d = x.shape
    a = pool.shape[0]
    row = pl.BlockSpec((1, tm, d), lambda i, j: (i, j, 0))
    vec = pl.BlockSpec((1, 1, d), lambda i, j: (i, 0, 0))
    small = lambda shape: pl.BlockSpec(shape, lambda i, j: (0, 0))
    heads = pl.BlockSpec((1, a // LANES, tm, LANES), lambda i, j: (i, 0, j, 0))
    qkv_shape = jax.ShapeDtypeStruct((b, a // LANES, s, LANES), _BF16)
    return pl.pallas_call(
        _proj_kernel,
        grid=(b, s // tm),
        in_specs=[row, small((1, d)), vec, vec, _resident(w_in.shape),
                  _resident(pool.shape), small((1, a)), small((1, a)),
                  small((CONV_K, a)), small((1, a))],
        out_specs=[heads, heads, heads, pl.BlockSpec((1, tm, a), lambda i, j: (i, j, 0))],
        out_shape=[qkv_shape, qkv_shape, qkv_shape,
                   jax.ShapeDtypeStruct((b, s, a), _BF16)],
        scratch_shapes=[pltpu.VMEM((tm + 2 * SUBLANES, a), _F32)],
        compiler_params=_params(2),
        name="mixer_in_proj",
    )(x, g, shift, scale, w_in, pool, qg, kg, conv_w, conv_b)


def _attn_unit(qb, kprev, kcur, vprev, vcur, prev_valid):
    n = ATTN_BLOCK
    lane = lax.broadcasted_iota(jnp.int32, (n, LANES), 1)
    head0 = lane < HEAD_DIM
    zero = jnp.zeros_like(qb)
    q2 = jnp.concatenate([jnp.where(head0, qb, zero), jnp.where(head0, zero, qb)], axis=0)
    kk = jnp.concatenate([kprev, kcur], axis=0)
    s = lax.dot_general(q2, kk, (((1,), (1,)), ((), ())),
                        preferred_element_type=_F32)
    qi = lax.broadcasted_iota(jnp.int32, (2 * n, n), 0) & (n - 1)
    kj = lax.broadcasted_iota(jnp.int32, (2 * n, n), 1)
    s_prev = jnp.where((kj >= qi) & prev_valid, s[:, :n], NEG)
    s_cur = jnp.where(kj <= qi, s[:, n:], NEG)
    m = jnp.max(jnp.maximum(s_prev, s_cur), axis=-1, keepdims=True)
    p = jnp.concatenate([jnp.exp(s_prev - m), jnp.exp(s_cur - m)], axis=1).astype(_BF16)
    vv = jnp.concatenate([vprev, vcur], axis=0)
    vext = jnp.concatenate([vv, jnp.ones_like(vv)], axis=1)
    o = jnp.dot(p, vext, preferred_element_type=_F32)
    acc = jnp.where(head0, o[:n, :LANES], o[n:, :LANES])
    den = jnp.where(head0, o[:n, LANES:], o[n:, LANES:])
    mx = jnp.where(head0, jnp.broadcast_to(m[:n], (n, LANES)),
                   jnp.broadcast_to(m[n:], (n, LANES)))
    return mx, den, acc


def _attn_kernel(*refs):
    nb = len(DILATIONS)
    q_refs = refs[0:nb]
    k_refs = refs[nb:2 * nb]
    kh_refs = refs[2 * nb:3 * nb]
    v_refs = refs[3 * nb:4 * nb]
    vh_refs = refs[4 * nb:5 * nb]
    y_ref, m_run, den_run, num_run = refs[5 * nb:]
    n = ATTN_BLOCK
    not_first_tile = pl.program_id(2) > 0

    for bi, d in enumerate(DILATIONS):
        q_ref, k_ref, kh_ref, v_ref, vh_ref = (
            q_refs[bi], k_refs[bi], kh_refs[bi], v_refs[bi], vh_refs[bi])
        n_blk = ATTN_TILE // (d * n)

        def rows_of(start, d=d):
            return pl.ds(start, n, stride=d) if d > 1 else pl.ds(start, n)

        def merge(r, j, stats, first_branch=(bi == 0), d=d, rows_of=rows_of):
            mx, den, acc = stats
            rows = rows_of(j * (n * d) + r)
            if first_branch:
                m_run[rows, :] = mx
                den_run[rows, :] = den
                num_run[rows, :] = acc
            else:
                m_old = m_run[rows, :]
                m_new = jnp.maximum(m_old, mx)
                w_old = jnp.exp(m_old - m_new)
                w_new = jnp.exp(mx - m_new)
                m_run[rows, :] = m_new
                den_run[rows, :] = w_old * den_run[rows, :] + w_new * den
                num_run[rows, :] = w_old * num_run[rows, :] + w_new * acc

        def residue(r, carry, q_ref=q_ref, k_ref=k_ref, kh_ref=kh_ref, v_ref=v_ref,
                    vh_ref=vh_ref, n_blk=n_blk, merge=merge):
            cols = pl.ds(pl.multiple_of(r * LANES, LANES), LANES)
            merge(r, 0, _attn_unit(q_ref[0:n, cols], kh_ref[:, cols], k_ref[0:n, cols],
                                   vh_ref[:, cols], v_ref[0:n, cols], not_first_tile))

            def block(j, c):
                cur = pl.ds(pl.multiple_of(j * n, n), n)
                prev = pl.ds(pl.multiple_of((j - 1) * n, n), n)
                merge(r, j, _attn_unit(q_ref[cur, cols], k_ref[prev, cols], k_ref[cur, cols],
                                       v_ref[prev, cols], v_ref[cur, cols], True))
                return c

            if n_blk > 1:
                lax.fori_loop(1, n_blk, block, 0)
            return carry

        if d > 1:
            lax.fori_loop(0, d, residue, 0)
        else:
            residue(0, 0)

    y_ref[...] = (num_run[...] / den_run[...]).astype(y_ref.dtype)


def _dilated_attention(q, k, v):
    b, hp, s, _ = q.shape
    n = ATTN_BLOCK

    def view(t, d):
        return t.reshape(b, hp, s // d, d * LANES)

    def tile_spec(d):
        return pl.BlockSpec((None, None, ATTN_TILE // d, d * LANES),
                            lambda i, h, t: (i, h, t, 0))

    def halo_spec(d):
        per_tile = ATTN_TILE // (d * n)
        return pl.BlockSpec((None, None, n, d * LANES),
                            lambda i, h, t: (i, h, jnp.maximum(t * per_tile - 1, 0), 0))

    tiles = [tile_spec(d) for d in DILATIONS]
    halos = [halo_spec(d) for d in DILATIONS]
    args = ([view(q, d) for d in DILATIONS]
            + [view(k, d) for d in DILATIONS] * 2
            + [view(v, d) for d in DILATIONS] * 2)
    return pl.pallas_call(
        _attn_kernel,
        grid=(b, hp, s // ATTN_TILE),
        in_specs=tiles + tiles + halos + tiles + halos,
        out_specs=pl.BlockSpec((None, ATTN_TILE, LANES), lambda i, h, t: (i, t, h)),
        out_shape=jax.ShapeDtypeStruct((b, s, hp * LANES), _BF16),
        scratch_shapes=[pltpu.VMEM((ATTN_TILE, LANES), _F32)] * 3,
        compiler_params=_params(3),
        name="dilated_attention",
    )(*args)


def _out_kernel(x_ref, ya_ref, yc_ref, gate_ref, w_ref, o_ref):
    a = ya_ref.shape[2]
    y = (jnp.dot(ya_ref[0], w_ref[0:a, :], preferred_element_type=_F32)
         + jnp.dot(yc_ref[0], w_ref[a:, :], preferred_element_type=_F32))
    o_ref[0] = x_ref[0] + gate_ref[0] * y


def _mixer_out(x, ya, yc, gate, w_out, tm=1024):
    b, s, d = x.shape
    a = ya.shape[2]
    row = pl.BlockSpec((1, tm, d), lambda i, j: (i, j, 0))
    half = pl.BlockSpec((1, tm, a), lambda i, j: (i, j, 0))
    return pl.pallas_call(
        _out_kernel,
        grid=(b, s // tm),
        in_specs=[row, half, half, pl.BlockSpec((1, 1, d), lambda i, j: (i, 0, 0)),
                  _resident(w_out.shape)],
        out_specs=row,
        out_shape=jax.ShapeDtypeStruct(x.shape, x.dtype),
        compiler_params=_params(2),
        name="mixer_out_proj",
    )(x, ya, yc, gate, w_out)


def kernel(x, c, w_ada, b_ada, norm_g, w_in, q_norm_g, k_norm_g, conv_w, conv_b,
           w_out, ffn_w1, ffn_w2):
    b, s, d = x.shape
    depth = w_ada.shape[0]
    a = conv_w.shape[-1]
    n_heads = a // HEAD_DIM
    assert d - a == a and a % LANES == 0 and s % ATTN_TILE == 0
    assert w_ada.shape[-1] == N_SUB * N_MOD * d

    rows = -(-b // SUBLANES) * SUBLANES
    c_pad = jnp.pad(c, ((0, rows - b), (0, 0)))
    mod = _modulation(c_pad, w_ada, b_ada)[:, :b].reshape(depth, b, N_SUB, N_MOD, 1, d)

    head_id = jnp.arange(a) // HEAD_DIM
    pool = jnp.where(head_id[:, None] == head_id[None, :], 1.0 / HEAD_DIM, 0.0).astype(_BF16)

    w1 = ffn_w1.astype(_BF16)
    w2 = ffn_w2.astype(_BF16)
    w_in_b = w_in.astype(_BF16)
    w_out_b = w_out.astype(_BF16)

    for layer in range(depth):
        shift = lambda i: mod[layer, :, i, 0]
        scale = lambda i: mod[layer, :, i, 1]
        gate = lambda i: mod[layer, :, i, 2]
        g = lambda i: norm_g[layer, i][None, :]

        x = _ffn(x, g(0), shift(0), scale(0), gate(0), w1[layer, 0], w2[layer, 0])
        q, k, v, yc = _mixer_proj(
            x, g(1), shift(1), scale(1), w_in_b[layer], pool,
            jnp.tile(q_norm_g[layer], n_heads)[None, :],
            jnp.tile(k_norm_g[layer], n_heads)[None, :],
            conv_w[layer], conv_b[layer][None, :])
        ya = _dilated_attention(q, k, v)
        x = _mixer_out(x, ya, yc, gate(1), w_out_b[layer])
        x = _ffn(x, g(2), shift(2), scale(2), gate(2), w1[layer, 1], w2[layer, 1])
    return x
```

```python
import functools

import jax
import jax.numpy as jnp
from jax import lax
from jax.experimental import pallas as pl
from jax.experimental.pallas import tpu as pltpu

EPS = 1e-6
NEG = -1e30
HEAD_DIM = 64
N_SUB = 3
N_MOD = 3
CONV_K = 3
DILATIONS = (1, 4, 16)
ATTN_BLOCK = 128
LANES = 128
SUBLANES = 8
ATTN_TILE = ATTN_BLOCK * DILATIONS[-1]
ATTN_GROUP = 8
VMEM_LIMIT_BYTES = 56 * 1024 * 1024

_BF16 = jnp.bfloat16
_F32 = jnp.float32


def _params(n_axes):
    return pltpu.CompilerParams(
        dimension_semantics=("arbitrary",) * n_axes,
        vmem_limit_bytes=VMEM_LIMIT_BYTES)


def _resident(shape):
    nd = len(shape)
    return pl.BlockSpec(shape, lambda *_: (0,) * nd, pipeline_mode=pl.Buffered(1))


def _silu(z):
    return z * (1.0 / (1.0 + jnp.exp(-z)))


def _ada_norm(x, g, shift, scale):
    y = x * lax.rsqrt(jnp.mean(x * x, axis=-1, keepdims=True) + EPS)
    return (y * g) * (1.0 + scale) + shift


def _mod_kernel(c_ref, w_ref, b_ref, o_ref):
    a = _silu(c_ref[...]).astype(_BF16)
    o_ref[0] = jnp.dot(a, w_ref[0].astype(_BF16),
                       preferred_element_type=_F32) + b_ref[0]


def _modulation(c_pad, w_ada, b_ada):
    depth, d, n = w_ada.shape
    rows = c_pad.shape[0]
    tn = 1024
    return pl.pallas_call(
        _mod_kernel,
        grid=(depth, n // tn),
        in_specs=[
            pl.BlockSpec((rows, d), lambda l, j: (0, 0)),
            pl.BlockSpec((1, d, tn), lambda l, j: (l, 0, j)),
            pl.BlockSpec((1, 1, tn), lambda l, j: (l, 0, j)),
        ],
        out_specs=pl.BlockSpec((1, rows, tn), lambda l, j: (l, 0, j)),
        out_shape=jax.ShapeDtypeStruct((depth, rows, n), _F32),
        compiler_params=_params(2),
        name="ada_modulation",
    )(c_pad, w_ada, b_ada.reshape(depth, 1, n))


def _ffn_kernel(x_ref, g_ref, shift_ref, scale_ref, gate_ref, w1_ref, w2_ref, o_ref):
    x = x_ref[0]
    h = _ada_norm(x, g_ref[...], shift_ref[0], scale_ref[0]).astype(_BF16)
    gu = jnp.dot(h, w1_ref[...], preferred_element_type=_F32)
    f = w2_ref.shape[0]
    act = (_silu(gu[:, :f]) * gu[:, f:]).astype(_BF16)
    y = jnp.dot(act, w2_ref[...], preferred_element_type=_F32)
    o_ref[0] = x + (0.5 * gate_ref[0]) * y


def _ffn(x, g, shift, scale, gate, w1, w2, tm=512):
    b, s, d = x.shape
    row = pl.BlockSpec((1, tm, d), lambda i, j: (i, j, 0))
    vec = pl.BlockSpec((1, 1, d), lambda i, j: (i, 0, 0))
    return pl.pallas_call(
        _ffn_kernel,
        grid=(b, s // tm),
        in_specs=[row, pl.BlockSpec((1, d), lambda i, j: (0, 0)), vec, vec, vec,
                  _resident(w1.shape), _resident(w2.shape)],
        out_specs=row,
        out_shape=jax.ShapeDtypeStruct(x.shape, x.dtype),
        compiler_params=_params(2),
        name="swiglu_half_step",
    )(x, g, shift, scale, gate, w1, w2)


def _head_rmsnorm(z, gain, pool):
    zz = z * z
    hi = zz.astype(_BF16)
    lo = (zz - hi.astype(_F32)).astype(_BF16)
    ms = (jnp.dot(hi, pool, preferred_element_type=_F32)
          + jnp.dot(lo, pool, preferred_element_type=_F32))
    return (z * lax.rsqrt(ms + EPS)) * gain


def _store_dilated(z, out_refs, stage_refs):
    tm, a = z.shape
    n_hp = a // LANES
    for hp in range(n_hp):
        zc = z[:, hp * LANES:(hp + 1) * LANES]
        out_refs[0][0, hp] = zc.astype(_BF16)
        stage_refs[0][hp * tm:(hp + 1) * tm, :] = zc
    for i in range(1, len(DILATIONS)):
        d_prev, d = DILATIONS[i - 1], DILATIONS[i]
        step = d // d_prev
        rows_prev, rows = tm // d_prev, tm // d
        for hp in range(n_hp):
            for r_prev in range(d_prev):
                base = (hp * d_prev + r_prev) * rows_prev
                for sub in range(step):
                    r = r_prev + d_prev * sub
                    piece = stage_refs[i - 1][pl.ds(base + sub, rows, stride=step), :]
                    out_refs[i][0, hp, :, r * LANES:(r + 1) * LANES] = piece.astype(_BF16)
                    if i + 1 < len(DILATIONS):
                        dst = (hp * d + r) * rows
                        stage_refs[i][dst:dst + rows, :] = piece


def _proj_kernel(x_ref, g_ref, shift_ref, scale_ref, w_in_ref, pool_ref, qg_ref,
                 kg_ref, cw_ref, cb_ref, *refs):
    nb = len(DILATIONS)
    q_refs, k_refs, v_refs = refs[0:nb], refs[nb:2 * nb], refs[2 * nb:3 * nb]
    yc_ref, cv_ref = refs[3 * nb], refs[3 * nb + 1]
    stage_refs = refs[3 * nb + 2:]
    tm = x_ref.shape[1]
    a = pool_ref.shape[0]
    halo = SUBLANES

    @pl.when(pl.program_id(1) == 0)
    def _():
        cv_ref[0:halo, :] = jnp.zeros((halo, a), _F32)

    h = _ada_norm(x_ref[0], g_ref[...], shift_ref[0], scale_ref[0]).astype(_BF16)
    proj = jnp.dot(h, w_in_ref[...], preferred_element_type=_F32)
    pool = pool_ref[...]
    q = _head_rmsnorm(proj[:, 0:a], qg_ref[...], pool) * (HEAD_DIM ** -0.5)
    k = _head_rmsnorm(proj[:, a:2 * a], kg_ref[...], pool)
    _store_dilated(q, q_refs, stage_refs)
    _store_dilated(k, k_refs, stage_refs)
    _store_dilated(proj[:, 2 * a:3 * a], v_refs, stage_refs)

    gate_b = proj[:, 3 * a:4 * a]
    cv = proj[:, 4 * a:5 * a] * proj[:, 5 * a:6 * a]
    cv_ref[halo:halo + tm, :] = cv
    conv = (cw_ref[0:1, :] * cv_ref[halo - 2:halo - 2 + tm, :]
            + cw_ref[1:2, :] * cv_ref[halo - 1:halo - 1 + tm, :]
            + cw_ref[2:3, :] * cv)
    yc_ref[0] = (gate_b * (conv + cb_ref[...])).astype(_BF16)
    cv_ref[0:halo, :] = cv_ref[tm:tm + halo, :]


def _mixer_proj(x, g, shift, scale, w_in, pool, qg, kg, conv_w, conv_b, tm=512):
    b, s, d = x.shape
    a = pool.shape[0]
    row = pl.BlockSpec((1, tm, d), lambda i, j: (i, j, 0))
    vec = pl.BlockSpec((1, 1, d), lambda i, j: (i, 0, 0))
    small = lambda shape: pl.BlockSpec(shape, lambda i, j: (0, 0))
    n_hp = a // LANES
    heads = [pl.BlockSpec((1, n_hp, tm // dl, dl * LANES), lambda i, j: (i, 0, j, 0))
             for dl in DILATIONS]
    qkv_shapes = [jax.ShapeDtypeStruct((b, n_hp, s // dl, dl * LANES), _BF16)
                  for dl in DILATIONS]
    outs = pl.pallas_call(
        _proj_kernel,
        grid=(b, s // tm),
        in_specs=[row, small((1, d)), vec, vec, _resident(w_in.shape),
                  _resident(pool.shape), small((1, a)), small((1, a)),
                  small((CONV_K, a)), small((1, a))],
        out_specs=heads * 3 + [pl.BlockSpec((1, tm, a), lambda i, j: (i, j, 0))],
        out_shape=qkv_shapes * 3 + [jax.ShapeDtypeStruct((b, s, a), _BF16)],
        scratch_shapes=([pltpu.VMEM((tm + 2 * SUBLANES, a), _F32)]
                        + [pltpu.VMEM((n_hp * tm, LANES), _F32)] * (len(DILATIONS) - 1)),
        compiler_params=_params(2),
        name="mixer_in_proj",
    )(x, g, shift, scale, w_in, pool, qg, kg, conv_w, conv_b)
    nb = len(DILATIONS)
    return outs[0:nb], outs[nb:2 * nb], outs[2 * nb:3 * nb], outs[3 * nb]


def _attn_unit(qb, kprev, kcur, vprev, vcur, prev_valid):
    n = ATTN_BLOCK
    lane = lax.broadcasted_iota(jnp.int32, (n, LANES), 1)
    head0 = lane < HEAD_DIM
    zero = jnp.zeros_like(qb)
    q2 = jnp.concatenate([jnp.where(head0, qb, zero), jnp.where(head0, zero, qb)], axis=0)
    kk = jnp.concatenate([kprev, kcur], axis=0)
    s = lax.dot_general(q2, kk, (((1,), (1,)), ((), ())),
                        preferred_element_type=_F32)
    qi = lax.broadcasted_iota(jnp.int32, (2 * n, n), 0) & (n - 1)
    kj = lax.broadcasted_iota(jnp.int32, (2 * n, n), 1)
    s_prev = jnp.where((kj >= qi) & prev_valid, s[:, :n], NEG)
    s_cur = jnp.where(kj <= qi, s[:, n:], NEG)
    m = jnp.max(jnp.maximum(s_prev, s_cur), axis=-1, keepdims=True)
    p = jnp.concatenate([jnp.exp(s_prev - m), jnp.exp(s_cur - m)], axis=1).astype(_BF16)
    vv = jnp.concatenate([vprev, vcur], axis=0)
    vext = jnp.concatenate([vv, jnp.ones_like(vv)], axis=1)
    o = jnp.dot(p, vext, preferred_element_type=_F32)
    acc = jnp.where(head0, o[:n, :LANES], o[n:, :LANES])
    den = jnp.where(head0, o[:n, LANES:], o[n:, LANES:])
    mx = jnp.where(head0, jnp.broadcast_to(m[:n], (n, LANES)),
                   jnp.broadcast_to(m[n:], (n, LANES)))
    return mx, den, acc


def _aligned(x, m):
    return x if isinstance(x, int) else pl.multiple_of(x, m)


def _loop(lo, hi, body):
    if hi - lo == 1:
        body(lo)
    elif hi > lo:
        lax.fori_loop(lo, hi, lambda i, c: (body(i), c)[1], 0)


def _attn_kernel(*refs):
    nb = len(DILATIONS)
    q_refs = refs[0:nb]
    k_refs = refs[nb:2 * nb]
    kh_refs = refs[2 * nb:3 * nb]
    v_refs = refs[3 * nb:4 * nb]
    vh_refs = refs[4 * nb:5 * nb]
    y_ref = refs[5 * nb]
    stat_refs = refs[5 * nb + 1:]
    n = ATTN_BLOCK
    not_first_tile = pl.program_id(2) > 0

    for bi, d in enumerate(DILATIONS):
        q_ref, k_ref, kh_ref, v_ref, vh_ref = (
            q_refs[bi], k_refs[bi], kh_refs[bi], v_refs[bi], vh_refs[bi])
        m_ref, den_ref, num_ref = stat_refs[3 * bi:3 * bi + 3]
        n_blk = ATTN_TILE // (d * n)
        blk_per_iter = min(ATTN_GROUP, n_blk)
        res_per_iter = ATTN_GROUP // blk_per_iter

        def unit(j, r, d=d, q_ref=q_ref, k_ref=k_ref, kh_ref=kh_ref, v_ref=v_ref,
                 vh_ref=vh_ref, m_ref=m_ref, den_ref=den_ref, num_ref=num_ref):
            cols = pl.ds(_aligned(r * LANES, LANES), LANES)
            cur = pl.ds(_aligned(j * n, n), n)
            if isinstance(j, int) and j == 0:
                kp, vp, valid = kh_ref[:, cols], vh_ref[:, cols], not_first_tile
            else:
                prev = pl.ds(_aligned((j - 1) * n, n), n)
                kp, vp, valid = k_ref[prev, cols], v_ref[prev, cols], True
            mx, den, acc = _attn_unit(q_ref[cur, cols], kp, k_ref[cur, cols],
                                      vp, v_ref[cur, cols], valid)
            start = j * (n * d) + r
            rows = pl.ds(start, n, stride=d) if d > 1 else pl.ds(start, n)
            m_ref[rows, :] = mx
            den_ref[rows, :] = den
            num_ref[rows, :] = acc

        def residue_group(rg, unit=unit, n_blk=n_blk, blk_per_iter=blk_per_iter,
                          res_per_iter=res_per_iter):
            rs = [rg * res_per_iter + i for i in range(res_per_iter)]
            for r in rs:
                for j in range(blk_per_iter):
                    unit(j, r)

            def block_group(g):
                for r in rs:
                    for i in range(blk_per_iter):
                        unit(g * blk_per_iter + i, r)

            _loop(1, n_blk // blk_per_iter, block_group)

        _loop(0, d // res_per_iter, residue_group)

    def combine(c):
        rows = pl.ds(_aligned(c * n, n), n)
        ms = [stat_refs[3 * bi][rows, :] for bi in range(nb)]
        m = functools.reduce(jnp.maximum, ms)
        ws = [jnp.exp(mb - m) for mb in ms]
        den = sum(w * stat_refs[3 * bi + 1][rows, :] for bi, w in enumerate(ws))
        num = sum(w * stat_refs[3 * bi + 2][rows, :] for bi, w in enumerate(ws))
        y_ref[rows, :] = (num / den).astype(y_ref.dtype)

    _loop(0, ATTN_TILE // n, combine)


def _dilated_attention(qs, ks, vs):
    b, hp, s, _ = qs[0].shape
    n = ATTN_BLOCK

    def tile_spec(d):
        return pl.BlockSpec((None, None, ATTN_TILE // d, d * LANES),
                            lambda i, h, t: (i, h, t, 0))

    def halo_spec(d):
        per_tile = ATTN_TILE // (d * n)
        return pl.BlockSpec((None, None, n, d * LANES),
                            lambda i, h, t: (i, h, jnp.maximum(t * per_tile - 1, 0), 0))

    tiles = [tile_spec(d) for d in DILATIONS]
    halos = [halo_spec(d) for d in DILATIONS]
    args = list(qs) + list(ks) * 2 + list(vs) * 2
    return pl.pallas_call(
        _attn_kernel,
        grid=(b, hp, s // ATTN_TILE),
        in_specs=tiles + tiles + halos + tiles + halos,
        out_specs=pl.BlockSpec((None, ATTN_TILE, LANES), lambda i, h, t: (i, t, h)),
        out_shape=jax.ShapeDtypeStruct((b, s, hp * LANES), _BF16),
        scratch_shapes=[pltpu.VMEM((ATTN_TILE, LANES), _F32)] * (3 * len(DILATIONS)),
        compiler_params=_params(3),
        name="dilated_attention",
    )(*args)


def _out_kernel(x_ref, ya_ref, yc_ref, gate_ref, w_ref, o_ref):
    a = ya_ref.shape[2]
    y = (jnp.dot(ya_ref[0], w_ref[0:a, :], preferred_element_type=_F32)
         + jnp.dot(yc_ref[0], w_ref[a:, :], preferred_element_type=_F32))
    o_ref[0] = x_ref[0] + gate_ref[0] * y


def _mixer_out(x, ya, yc, gate, w_out, tm=1024):
    b, s, d = x.shape
    a = ya.shape[2]
    row = pl.BlockSpec((1, tm, d), lambda i, j: (i, j, 0))
    half = pl.BlockSpec((1, tm, a), lambda i, j: (i, j, 0))
    return pl.pallas_call(
        _out_kernel,
        grid=(b, s // tm),
        in_specs=[row, half, half, pl.BlockSpec((1, 1, d), lambda i, j: (i, 0, 0)),
                  _resident(w_out.shape)],
        out_specs=row,
        out_shape=jax.ShapeDtypeStruct(x.shape, x.dtype),
        compiler_params=_params(2),
        name="mixer_out_proj",
    )(x, ya, yc, gate, w_out)


def kernel(x, c, w_ada, b_ada, norm_g, w_in, q_norm_g, k_norm_g, conv_w, conv_b,
           w_out, ffn_w1, ffn_w2):
    b, s, d = x.shape
    depth = w_ada.shape[0]
    a = conv_w.shape[-1]
    n_heads = a // HEAD_DIM
    assert d - a == a and a % LANES == 0 and s % ATTN_TILE == 0
    assert w_ada.shape[-1] == N_SUB * N_MOD * d

    rows = -(-b // SUBLANES) * SUBLANES
    c_pad = jnp.pad(c, ((0, rows - b), (0, 0)))
    mod = _modulation(c_pad, w_ada, b_ada)[:, :b].reshape(depth, b, N_SUB, N_MOD, 1, d)

    head_id = jnp.arange(a) // HEAD_DIM
    pool = jnp.where(head_id[:, None] == head_id[None, :], 1.0 / HEAD_DIM, 0.0).astype(_BF16)

    w1 = ffn_w1.astype(_BF16)
    w2 = ffn_w2.astype(_BF16)
    w_in_b = w_in.astype(_BF16)
    w_out_b = w_out.astype(_BF16)

    for layer in range(depth):
        shift = lambda i: mod[layer, :, i, 0]
        scale = lambda i: mod[layer, :, i, 1]
        gate = lambda i: mod[layer, :, i, 2]
        g = lambda i: norm_g[layer, i][None, :]

        x = _ffn(x, g(0), shift(0), scale(0), gate(0), w1[layer, 0], w2[layer, 0])
        qs, ks, vs, yc = _mixer_proj(
            x, g(1), shift(1), scale(1), w_in_b[layer], pool,
            jnp.tile(q_norm_g[layer], n_heads)[None, :],
            jnp.tile(k_norm_g[layer], n_heads)[None, :],
            conv_w[layer], conv_b[layer][None, :])
        ya = _dilated_attention(qs, ks, vs)
        x = _mixer_out(x, ya, yc, gate(1), w_out_b[layer])
        x = _ffn(x, g(2), shift(2), scale(2), gate(2), w1[layer, 1], w2[layer, 1])
    return x
```

```python
import functools

import jax
import jax.numpy as jnp
from jax import lax
from jax.experimental import pallas as pl
from jax.experimental.pallas import tpu as pltpu

EPS = 1e-6
NEG = -1e30
LOG2_E = 1.4426950408889634
HEAD_DIM = 64
N_SUB = 3
N_MOD = 3
CONV_K = 3
DILATIONS = (1, 4, 16)
ATTN_BLOCK = 128
LANES = 128
SUBLANES = 8
ATTN_TILE = ATTN_BLOCK * DILATIONS[-1]
FFN_ROW_GROUPS = 2
VMEM_LIMIT_BYTES = 56 * 1024 * 1024

_BF16 = jnp.bfloat16
_F32 = jnp.float32


def _params(n_axes):
    return pltpu.CompilerParams(
        dimension_semantics=("arbitrary",) * n_axes,
        vmem_limit_bytes=VMEM_LIMIT_BYTES)


def _resident(shape, lead=()):
    nd = len(shape) - len(lead)
    return pl.BlockSpec((None,) * len(lead) + tuple(shape[len(lead):]),
                        lambda *_: tuple(lead) + (0,) * nd,
                        pipeline_mode=pl.Buffered(1))


def _silu(z):
    return z * (1.0 / (1.0 + jnp.exp(-z)))


def _ada_norm(x, g, shift, scale):
    y = x * lax.rsqrt(jnp.mean(x * x, axis=-1, keepdims=True) + EPS)
    return (y * g) * (1.0 + scale) + shift


def _mod_kernel(c_ref, w_ref, b_ref, o_ref):
    a = _silu(c_ref[...]).astype(_BF16)
    o_ref[0] = jnp.dot(a, w_ref[0].astype(_BF16),
                       preferred_element_type=_F32) + b_ref[0]


def _modulation(c_pad, w_ada, b_ada):
    depth, d, n = w_ada.shape
    rows = c_pad.shape[0]
    tn = 1024
    return pl.pallas_call(
        _mod_kernel,
        grid=(depth, n // tn),
        in_specs=[
            pl.BlockSpec((rows, d), lambda l, j: (0, 0)),
            pl.BlockSpec((1, d, tn), lambda l, j: (l, 0, j)),
            pl.BlockSpec((1, 1, tn), lambda l, j: (l, 0, j)),
        ],
        out_specs=pl.BlockSpec((1, rows, tn), lambda l, j: (l, 0, j)),
        out_shape=jax.ShapeDtypeStruct((depth, rows, n), _F32),
        compiler_params=_params(2),
        name="ada_modulation",
    )(c_pad, w_ada, b_ada.reshape(depth, 1, n))


def _ffn_kernel(x_ref, g_ref, shift_ref, scale_ref, gate_ref, w1_ref, w2_ref, o_ref):
    tm = x_ref.shape[1]
    f = w2_ref.shape[0]
    for lo in range(0, tm, tm // FFN_ROW_GROUPS):
        rows = slice(lo, lo + tm // FFN_ROW_GROUPS)
        x = x_ref[0, rows]
        h = _ada_norm(x, g_ref[...], shift_ref[0], scale_ref[0]).astype(_BF16)
        gu = jnp.dot(h, w1_ref[...], preferred_element_type=_F32)
        act = (_silu(gu[:, :f]) * gu[:, f:]).astype(_BF16)
        y = jnp.dot(act, w2_ref[...], preferred_element_type=_F32)
        o_ref[0, rows] = x + (0.5 * gate_ref[0]) * y


def _ffn(x, g, shift, scale, gate, w1, w2, lead, tm=512):
    b, s, d = x.shape
    row = pl.BlockSpec((1, tm, d), lambda i, j: (i, j, 0))
    vec = pl.BlockSpec((1, 1, d), lambda i, j: (i, 0, 0))
    return pl.pallas_call(
        _ffn_kernel,
        grid=(b, s // tm),
        in_specs=[row, pl.BlockSpec((1, d), lambda i, j: (0, 0)), vec, vec, vec,
                  _resident(w1.shape, lead), _resident(w2.shape, lead)],
        out_specs=row,
        out_shape=jax.ShapeDtypeStruct(x.shape, x.dtype),
        compiler_params=_params(2),
        name="swiglu_half_step",
    )(x, g, shift, scale, gate, w1, w2)


def _head_rmsnorm(z, gain, pool):
    zz = z * z
    hi = zz.astype(_BF16)
    lo = (zz - hi.astype(_F32)).astype(_BF16)
    ms = (jnp.dot(hi, pool, preferred_element_type=_F32)
          + jnp.dot(lo, pool, preferred_element_type=_F32))
    return (z * lax.rsqrt(ms + EPS)) * gain


def _store_dilated(z, out_refs, stage_refs):
    tm, a = z.shape
    n_hp = a // LANES
    for hp in range(n_hp):
        zc = z[:, hp * LANES:(hp + 1) * LANES]
        out_refs[0][0, hp] = zc.astype(_BF16)
        stage_refs[0][hp * tm:(hp + 1) * tm, :] = zc
    for i in range(1, len(DILATIONS)):
        d_prev, d = DILATIONS[i - 1], DILATIONS[i]
        step = d // d_prev
        rows_prev, rows = tm // d_prev, tm // d
        for hp in range(n_hp):
            for r_prev in range(d_prev):
                base = (hp * d_prev + r_prev) * rows_prev
                for sub in range(step):
                    r = r_prev + d_prev * sub
                    piece = stage_refs[i - 1][pl.ds(base + sub, rows, stride=step), :]
                    out_refs[i][0, hp, :, r * LANES:(r + 1) * LANES] = piece.astype(_BF16)
                    if i + 1 < len(DILATIONS):
                        dst = (hp * d + r) * rows
                        stage_refs[i][dst:dst + rows, :] = piece


def _proj_kernel(x_ref, g_ref, shift_ref, scale_ref, w_in_ref, pool_ref, qg_ref,
                 kg_ref, cw_ref, cb_ref, *refs):
    nb = len(DILATIONS)
    q_refs, k_refs, v_refs = refs[0:nb], refs[nb:2 * nb], refs[2 * nb:3 * nb]
    yc_ref, cv_ref = refs[3 * nb], refs[3 * nb + 1]
    stage_refs = refs[3 * nb + 2:]
    tm = x_ref.shape[1]
    a = pool_ref.shape[0]
    halo = SUBLANES

    @pl.when(pl.program_id(1) == 0)
    def _():
        cv_ref[0:halo, :] = jnp.zeros((halo, a), _F32)

    h = _ada_norm(x_ref[0], g_ref[...], shift_ref[0], scale_ref[0]).astype(_BF16)

    def proj(lo, hi):
        return jnp.dot(h, w_in_ref[:, lo * a:hi * a], preferred_element_type=_F32)

    pool = pool_ref[...]
    n_stage = nb - 1
    q = _head_rmsnorm(proj(0, 1), qg_ref[...], pool) * (HEAD_DIM ** -0.5 * LOG2_E)
    _store_dilated(q, q_refs, stage_refs[0:n_stage])
    k = _head_rmsnorm(proj(1, 2), kg_ref[...], pool)
    _store_dilated(k, k_refs, stage_refs[n_stage:2 * n_stage])
    _store_dilated(proj(2, 3), v_refs, stage_refs[2 * n_stage:3 * n_stage])

    conv_in = proj(3, 6)
    gate_b = conv_in[:, 0:a]
    cv = conv_in[:, a:2 * a] * conv_in[:, 2 * a:3 * a]
    cv_ref[halo:halo + tm, :] = cv
    conv = (cw_ref[0:1, :] * cv_ref[halo - 2:halo - 2 + tm, :]
            + cw_ref[1:2, :] * cv_ref[halo - 1:halo - 1 + tm, :]
            + cw_ref[2:3, :] * cv)
    yc_ref[0] = (gate_b * (conv + cb_ref[...])).astype(_BF16)
    cv_ref[0:halo, :] = cv_ref[tm:tm + halo, :]


def _mixer_proj(x, g, shift, scale, w_in, lead, pool, qg, kg, conv_w, conv_b, tm=512):
    b, s, d = x.shape
    a = pool.shape[0]
    row = pl.BlockSpec((1, tm, d), lambda i, j: (i, j, 0))
    vec = pl.BlockSpec((1, 1, d), lambda i, j: (i, 0, 0))
    small = lambda shape: pl.BlockSpec(shape, lambda i, j: (0, 0))
    n_hp = a // LANES
    heads = [pl.BlockSpec((1, n_hp, tm // dl, dl * LANES), lambda i, j: (i, 0, j, 0))
             for dl in DILATIONS]
    qkv_shapes = [jax.ShapeDtypeStruct((b, n_hp, s // dl, dl * LANES), _BF16)
                  for dl in DILATIONS]
    outs = pl.pallas_call(
        _proj_kernel,
        grid=(b, s // tm),
        in_specs=[row, small((1, d)), vec, vec, _resident(w_in.shape, lead),
                  _resident(pool.shape), small((1, a)), small((1, a)),
                  small((CONV_K, a)), small((1, a))],
        out_specs=heads * 3 + [pl.BlockSpec((1, tm, a), lambda i, j: (i, j, 0))],
        out_shape=qkv_shapes * 3 + [jax.ShapeDtypeStruct((b, s, a), _BF16)],
        scratch_shapes=([pltpu.VMEM((tm + 2 * SUBLANES, a), _F32)]
                        + [pltpu.VMEM((n_hp * tm, LANES), _F32)] * (3 * (len(DILATIONS) - 1))),
        compiler_params=_params(2),
        name="mixer_in_proj",
    )(x, g, shift, scale, w_in, pool, qg, kg, conv_w, conv_b)
    nb = len(DILATIONS)
    return outs[0:nb], outs[nb:2 * nb], outs[2 * nb:3 * nb], outs[3 * nb]


def _band_bias():
    n = ATTN_BLOCK
    qi = jnp.arange(n)[:, None]
    kj = jnp.arange(2 * n)[None, :]
    dist = qi + n - kj
    band = (dist >= 0) & (dist <= n)
    masks = jnp.stack([band, band & (kj >= n)])
    return jnp.where(masks, 0.0, NEG).astype(_F32)


def _attn_unit(qb, kprev, kcur, vprev, vcur, bias):
    n = ATTN_BLOCK
    lane = lax.broadcasted_iota(jnp.int32, (n, LANES), 1)
    head0 = lane < HEAD_DIM
    zero = jnp.zeros_like(qb)
    q2 = jnp.concatenate([jnp.where(head0, qb, zero), jnp.where(head0, zero, qb)], axis=0)
    kk = jnp.concatenate([kprev, kcur], axis=0)
    s = lax.dot_general(q2, kk, (((1,), (1,)), ((), ())),
                        preferred_element_type=_F32)
    s = jnp.concatenate([s[:n] + bias, s[n:] + bias], axis=0)
    m = jnp.max(s, axis=-1, keepdims=True)
    p = jnp.exp2(s - m).astype(_BF16)
    vv = jnp.concatenate([vprev, vcur], axis=0)
    vext = jnp.concatenate([vv, jnp.ones_like(vv)], axis=1)
    o = jnp.dot(p, vext, preferred_element_type=_F32)
    acc = jnp.where(head0, o[:n, :LANES], o[n:, :LANES])
    den = jnp.where(head0, o[:n, LANES:], o[n:, LANES:])
    mx = jnp.where(head0, jnp.broadcast_to(m[:n], (n, LANES)),
                   jnp.broadcast_to(m[n:], (n, LANES)))
    return mx, den, acc


def _attn_kernel(*refs):
    nb = len(DILATIONS)
    q_refs, k_refs, kh_refs, v_refs, vh_refs = (refs[i * nb:(i + 1) * nb] for i in range(5))
    bias_ref, y_ref = refs[5 * nb], refs[5 * nb + 1]
    stat_refs = refs[5 * nb + 2:]
    n = ATTN_BLOCK
    halo_slab = jnp.where(pl.program_id(2) > 0, 0, 1)

    def block_stats(bi, j, r):
        q_ref, k_ref, kh_ref, v_ref, vh_ref = (
            q_refs[bi], k_refs[bi], kh_refs[bi], v_refs[bi], vh_refs[bi])
        cols = slice(r * LANES, (r + 1) * LANES)
        cur = slice(j * n, (j + 1) * n)
        if j == 0:
            kp, vp, bias = kh_ref[:, cols], vh_ref[:, cols], bias_ref[halo_slab]
        else:
            prev = slice((j - 1) * n, j * n)
            kp, vp, bias = k_ref[prev, cols], v_ref[prev, cols], bias_ref[0]
        return _attn_unit(q_ref[cur, cols], kp, k_ref[cur, cols], vp, v_ref[cur, cols], bias)

    for bi in range(nb - 1, 0, -1):
        d = DILATIONS[bi]
        for r in range(d):
            for j in range(ATTN_TILE // (d * n)):
                rows = pl.ds(j * n * d + r, n, stride=d)
                for ref, val in zip(stat_refs[3 * (bi - 1):3 * bi], block_stats(bi, j, r)):
                    ref[rows, :] = val

    for j in range(ATTN_TILE // n):
        rows = slice(j * n, (j + 1) * n)
        stats = [block_stats(0, j, 0)] + [
            tuple(ref[rows, :] for ref in stat_refs[3 * (bi - 1):3 * bi]) for bi in range(1, nb)]
        m = functools.reduce(jnp.maximum, [st[0] for st in stats])
        ws = [jnp.exp2(st[0] - m) for st in stats]
        den = sum(w * st[1] for w, st in zip(ws, stats))
        num = sum(w * st[2] for w, st in zip(ws, stats))
        y_ref[rows, :] = (num / den).astype(y_ref.dtype)


def _dilated_attention(qs, ks, vs):
    b, hp, s, _ = qs[0].shape
    n = ATTN_BLOCK

    def tile_spec(d):
        return pl.BlockSpec((None, None, ATTN_TILE // d, d * LANES),
                            lambda i, h, t: (i, h, t, 0))

    def halo_spec(d):
        per_tile = ATTN_TILE // (d * n)
        return pl.BlockSpec((None, None, n, d * LANES),
                            lambda i, h, t: (i, h, jnp.maximum(t * per_tile - 1, 0), 0))

    tiles = [tile_spec(d) for d in DILATIONS]
    halos = [halo_spec(d) for d in DILATIONS]
    args = list(qs) + list(ks) * 2 + list(vs) * 2 + [_band_bias()]
    return pl.pallas_call(
        _attn_kernel,
        grid=(b, hp, s // ATTN_TILE),
        in_specs=tiles + tiles + halos + tiles + halos + [_resident((2, n, 2 * n))],
        out_specs=pl.BlockSpec((None, ATTN_TILE, LANES), lambda i, h, t: (i, t, h)),
        out_shape=jax.ShapeDtypeStruct((b, s, hp * LANES), _BF16),
        scratch_shapes=[pltpu.VMEM((ATTN_TILE, LANES), _F32)] * (3 * (len(DILATIONS) - 1)),
        compiler_params=_params(3),
        name="dilated_attention",
    )(*args)


def _out_kernel(x_ref, ya_ref, yc_ref, gate_ref, w_ref, o_ref):
    a = ya_ref.shape[2]
    y = (jnp.dot(ya_ref[0], w_ref[0:a, :], preferred_element_type=_F32)
         + jnp.dot(yc_ref[0], w_ref[a:, :], preferred_element_type=_F32))
    o_ref[0] = x_ref[0] + gate_ref[0] * y


def _mixer_out(x, ya, yc, gate, w_out, lead, tm=1024):
    b, s, d = x.shape
    a = ya.shape[2]
    row = pl.BlockSpec((1, tm, d), lambda i, j: (i, j, 0))
    half = pl.BlockSpec((1, tm, a), lambda i, j: (i, j, 0))
    return pl.pallas_call(
        _out_kernel,
        grid=(b, s // tm),
        in_specs=[row, half, half, pl.BlockSpec((1, 1, d), lambda i, j: (i, 0, 0)),
                  _resident(w_out.shape, lead)],
        out_specs=row,
        out_shape=jax.ShapeDtypeStruct(x.shape, x.dtype),
        compiler_params=_params(2),
        name="mixer_out_proj",
    )(x, ya, yc, gate, w_out)


def kernel(x, c, w_ada, b_ada, norm_g, w_in, q_norm_g, k_norm_g, conv_w, conv_b,
           w_out, ffn_w1, ffn_w2):
    b, s, d = x.shape
    depth = w_ada.shape[0]
    a = conv_w.shape[-1]
    n_heads = a // HEAD_DIM
    assert d - a == a and a % LANES == 0 and s % ATTN_TILE == 0 and DILATIONS[0] == 1
    assert w_ada.shape[-1] == N_SUB * N_MOD * d

    rows = -(-b // SUBLANES) * SUBLANES
    c_pad = jnp.pad(c, ((0, rows - b), (0, 0)))
    mod = _modulation(c_pad, w_ada, b_ada)[:, :b].reshape(depth, b, N_SUB, N_MOD, 1, d)

    head_id = jnp.arange(a) // HEAD_DIM
    pool = jnp.where(head_id[:, None] == head_id[None, :], 1.0 / HEAD_DIM, 0.0).astype(_BF16)

    w1 = ffn_w1.astype(_BF16)
    w2 = ffn_w2.astype(_BF16)
    w_in_b = w_in.astype(_BF16)
    w_out_b = w_out.astype(_BF16)

    for layer in range(depth):
        shift = lambda i: mod[layer, :, i, 0]
        scale = lambda i: mod[layer, :, i, 1]
        gate = lambda i: mod[layer, :, i, 2]
        g = lambda i: norm_g[layer, i][None, :]

        x = _ffn(x, g(0), shift(0), scale(0), gate(0), w1, w2, (layer, 0))
        qs, ks, vs, yc = _mixer_proj(
            x, g(1), shift(1), scale(1), w_in_b, (layer,), pool,
            jnp.tile(q_norm_g[layer], n_heads)[None, :],
            jnp.tile(k_norm_g[layer], n_heads)[None, :],
            conv_w[layer], conv_b[layer][None, :])
        ya = _dilated_attention(qs, ks, vs)
        x = _mixer_out(x, ya, yc, gate(1), w_out_b, (layer,))
        x = _ffn(x, g(2), shift(2), scale(2), gate(2), w1, w2, (layer, 1))
    return x
```

```python
import functools

import jax
import jax.numpy as jnp
from jax import lax
from jax.experimental import pallas as pl
from jax.experimental.pallas import tpu as pltpu

EPS = 1e-6
NEG = -1e30
LOG2_E = 1.4426950408889634
HEAD_DIM = 64
N_SUB = 3
N_MOD = 3
CONV_K = 3
DILATIONS = (1, 4, 16)
ATTN_BLOCK = 128
LANES = 128
SUBLANES = 8
MXU_DIM = 256
ATTN_TILE = ATTN_BLOCK * DILATIONS[-1]
FFN_ROW_GROUPS = 2
VMEM_LIMIT_BYTES = 56 * 1024 * 1024

_BF16 = jnp.bfloat16
_F32 = jnp.float32


def _params(n_axes):
    return pltpu.CompilerParams(
        dimension_semantics=("arbitrary",) * n_axes,
        vmem_limit_bytes=VMEM_LIMIT_BYTES)


def _resident(shape, lead=()):
    nd = len(shape) - len(lead)
    return pl.BlockSpec((None,) * len(lead) + tuple(shape[len(lead):]),
                        lambda *_: tuple(lead) + (0,) * nd,
                        pipeline_mode=pl.Buffered(1))


def _silu(z):
    return z * (1.0 / (1.0 + jnp.exp(-z)))


def _ada_norm(x, g, shift, scale):
    y = x * lax.rsqrt(jnp.mean(x * x, axis=-1, keepdims=True) + EPS)
    return (y * g) * (1.0 + scale) + shift


def _mod_kernel(c_ref, w_ref, b_ref, o_ref):
    a = _silu(c_ref[...]).astype(_BF16)
    o_ref[0] = jnp.dot(a, w_ref[0].astype(_BF16),
                       preferred_element_type=_F32) + b_ref[0]


def _modulation(c_pad, w_ada, b_ada):
    depth, d, n = w_ada.shape
    rows = c_pad.shape[0]
    tn = 1024
    return pl.pallas_call(
        _mod_kernel,
        grid=(depth, n // tn),
        in_specs=[
            pl.BlockSpec((rows, d), lambda l, j: (0, 0)),
            pl.BlockSpec((1, d, tn), lambda l, j: (l, 0, j)),
            pl.BlockSpec((1, 1, tn), lambda l, j: (l, 0, j)),
        ],
        out_specs=pl.BlockSpec((1, rows, tn), lambda l, j: (l, 0, j)),
        out_shape=jax.ShapeDtypeStruct((depth, rows, n), _F32),
        compiler_params=_params(2),
        name="ada_modulation",
    )(c_pad, w_ada, b_ada.reshape(depth, 1, n))


def _ffn_kernel(x_ref, g_ref, shift_ref, scale_ref, gate_ref, w1_ref, w2_ref, *refs):
    o_ref = refs[-1]
    tm = x_ref.shape[1]
    f = w2_ref.shape[0]
    for lo in range(0, tm, tm // FFN_ROW_GROUPS):
        rows = slice(lo, lo + tm // FFN_ROW_GROUPS)
        x = x_ref[0, rows]
        if len(refs) > 1:
            ya_ref, yc_ref, gate_mix_ref, wo_ref = refs[:-1]
            a = ya_ref.shape[2]
            mix = (jnp.dot(ya_ref[0, rows], wo_ref[0:a, :], preferred_element_type=_F32)
                   + jnp.dot(yc_ref[0, rows], wo_ref[a:, :], preferred_element_type=_F32))
            x = x + gate_mix_ref[0] * mix
        h = _ada_norm(x, g_ref[...], shift_ref[0], scale_ref[0]).astype(_BF16)
        gu = jnp.dot(h, w1_ref[...], preferred_element_type=_F32)
        act = (_silu(gu[:, :f]) * gu[:, f:]).astype(_BF16)
        y = jnp.dot(act, w2_ref[...], preferred_element_type=_F32)
        o_ref[0, rows] = x + (0.5 * gate_ref[0]) * y


def _ffn(x, g, shift, scale, gate, w1, w2, lead, mixer=None, tm=512):
    b, s, d = x.shape
    row = pl.BlockSpec((1, tm, d), lambda i, j: (i, j, 0))
    vec = pl.BlockSpec((1, 1, d), lambda i, j: (i, 0, 0))
    in_specs = [row, pl.BlockSpec((1, d), lambda i, j: (0, 0)), vec, vec, vec,
                _resident(w1.shape, lead), _resident(w2.shape, lead)]
    args = [x, g, shift, scale, gate, w1, w2]
    if mixer is not None:
        ya, yc, gate_mix, w_out, w_out_lead = mixer
        half = pl.BlockSpec((1, tm, ya.shape[2]), lambda i, j: (i, j, 0))
        in_specs += [half, half, vec, _resident(w_out.shape, w_out_lead)]
        args += [ya, yc, gate_mix, w_out]
    return pl.pallas_call(
        _ffn_kernel,
        grid=(b, s // tm),
        in_specs=in_specs,
        out_specs=row,
        out_shape=jax.ShapeDtypeStruct(x.shape, x.dtype),
        compiler_params=_params(2),
        name="swiglu_half_step",
    )(*args)


def _head_rmsnorm(z, gain, pool):
    zz = z * z
    hi = zz.astype(_BF16)
    lo = (zz - hi.astype(_F32)).astype(_BF16)
    width = pool.shape[0]
    ms = jnp.concatenate(
        [jnp.dot(hi[:, c:c + width], pool, preferred_element_type=_F32)
         + jnp.dot(lo[:, c:c + width], pool, preferred_element_type=_F32)
         for c in range(0, z.shape[1], width)], axis=1)
    return (z * lax.rsqrt(ms + EPS)) * gain


def _store_dilated(z, out_refs, stage_refs):
    tm, a = z.shape
    n_hp = a // LANES
    for hp in range(n_hp):
        zc = z[:, hp * LANES:(hp + 1) * LANES]
        out_refs[0][0, hp] = zc.astype(_BF16)
        stage_refs[0][hp * tm:(hp + 1) * tm, :] = zc
    for i in range(1, len(DILATIONS)):
        d_prev, d = DILATIONS[i - 1], DILATIONS[i]
        step = d // d_prev
        rows_prev, rows = tm // d_prev, tm // d
        for hp in range(n_hp):
            for r_prev in range(d_prev):
                base = (hp * d_prev + r_prev) * rows_prev
                for sub in range(step):
                    r = r_prev + d_prev * sub
                    piece = stage_refs[i - 1][pl.ds(base + sub, rows, stride=step), :]
                    out_refs[i][0, hp, :, r * LANES:(r + 1) * LANES] = piece.astype(_BF16)
                    if i + 1 < len(DILATIONS):
                        dst = (hp * d + r) * rows
                        stage_refs[i][dst:dst + rows, :] = piece


def _proj_kernel(x_ref, g_ref, shift_ref, scale_ref, w_in_ref, pool_ref, qg_ref,
                 kg_ref, cw_ref, cb_ref, *refs):
    nb = len(DILATIONS)
    q_refs, k_refs, v_refs = refs[0:nb], refs[nb:2 * nb], refs[2 * nb:3 * nb]
    yc_ref, cv_ref = refs[3 * nb], refs[3 * nb + 1]
    stage_refs = refs[3 * nb + 2:]
    tm = x_ref.shape[1]
    a = yc_ref.shape[2]
    halo = SUBLANES

    @pl.when(pl.program_id(1) == 0)
    def _():
        cv_ref[...] = jnp.zeros((halo, a), _F32)

    h = _ada_norm(x_ref[0], g_ref[...], shift_ref[0], scale_ref[0]).astype(_BF16)

    def proj(lo, hi):
        return jnp.dot(h, w_in_ref[:, lo * a:hi * a], preferred_element_type=_F32)

    conv_in = proj(3, 6)
    gate_b = conv_in[:, 0:a]
    cv = conv_in[:, a:2 * a] * conv_in[:, 2 * a:3 * a]
    ext = jnp.concatenate([cv_ref[...], cv], axis=0)
    conv = cw_ref[CONV_K - 1:CONV_K, :] * cv
    for back in range(1, CONV_K):
        tap = cw_ref[CONV_K - 1 - back:CONV_K - back, :]
        conv = conv + tap * pltpu.roll(ext, back, axis=0)[halo:]
    yc_ref[0] = (gate_b * (conv + cb_ref[...])).astype(_BF16)
    cv_ref[...] = cv[tm - halo:]

    pool = pool_ref[...]
    n_stage = nb - 1
    q = _head_rmsnorm(proj(0, 1), qg_ref[...], pool) * (HEAD_DIM ** -0.5 * LOG2_E)
    _store_dilated(q, q_refs, stage_refs[0:n_stage])
    k = _head_rmsnorm(proj(1, 2), kg_ref[...], pool)
    _store_dilated(k, k_refs, stage_refs[n_stage:2 * n_stage])
    _store_dilated(proj(2, 3), v_refs, stage_refs[2 * n_stage:3 * n_stage])


def _mixer_proj(x, g, shift, scale, w_in, lead, pool, qg, kg, conv_w, conv_b, tm=512):
    b, s, d = x.shape
    a = conv_w.shape[-1]
    row = pl.BlockSpec((1, tm, d), lambda i, j: (i, j, 0))
    vec = pl.BlockSpec((1, 1, d), lambda i, j: (i, 0, 0))
    small = lambda shape: pl.BlockSpec(shape, lambda i, j: (0, 0))
    n_hp = a // LANES
    heads = [pl.BlockSpec((1, n_hp, tm // dl, dl * LANES), lambda i, j: (i, 0, j, 0))
             for dl in DILATIONS]
    qkv_shapes = [jax.ShapeDtypeStruct((b, n_hp, s // dl, dl * LANES), _BF16)
                  for dl in DILATIONS]
    outs = pl.pallas_call(
        _proj_kernel,
        grid=(b, s // tm),
        in_specs=[row, small((1, d)), vec, vec, _resident(w_in.shape, lead),
                  _resident(pool.shape), small((1, a)), small((1, a)),
                  small((CONV_K, a)), small((1, a))],
        out_specs=heads * 3 + [pl.BlockSpec((1, tm, a), lambda i, j: (i, j, 0))],
        out_shape=qkv_shapes * 3 + [jax.ShapeDtypeStruct((b, s, a), _BF16)],
        scratch_shapes=([pltpu.VMEM((SUBLANES, a), _F32)]
                        + [pltpu.VMEM((n_hp * tm, LANES), _F32)] * (3 * (len(DILATIONS) - 1))),
        compiler_params=_params(2),
        name="mixer_in_proj",
    )(x, g, shift, scale, w_in, pool, qg, kg, conv_w, conv_b)
    nb = len(DILATIONS)
    return outs[0:nb], outs[nb:2 * nb], outs[2 * nb:3 * nb], outs[3 * nb]


def _band_bias():
    n = ATTN_BLOCK
    qi = jnp.arange(n)[:, None]
    kj = jnp.arange(2 * n)[None, :]
    dist = qi + n - kj
    band = (dist >= 0) & (dist <= n)
    masks = jnp.stack([band, band & (kj >= n)])
    return jnp.where(masks, 0.0, NEG).astype(_F32)


def _attn_unit(qb, kprev, kcur, vprev, vcur, bias):
    n = ATTN_BLOCK
    lane = lax.broadcasted_iota(jnp.int32, (n, LANES), 1)
    head0 = lane < HEAD_DIM
    zero = jnp.zeros_like(qb)
    q2 = jnp.concatenate([jnp.where(head0, qb, zero), jnp.where(head0, zero, qb)], axis=0)
    kk = jnp.concatenate([kprev, kcur], axis=0)
    s = lax.dot_general(q2, kk, (((1,), (1,)), ((), ())),
                        preferred_element_type=_F32)
    s = jnp.concatenate([s[:n] + bias, s[n:] + bias], axis=0)
    m = jnp.max(s, axis=-1, keepdims=True)
    p = jnp.exp2(s - m).astype(_BF16)
    vv = jnp.concatenate([vprev, vcur], axis=0)
    vext = jnp.concatenate([vv, jnp.ones_like(vv)], axis=1)
    o = jnp.dot(p, vext, preferred_element_type=_F32)
    acc = jnp.where(head0, o[:n, :LANES], o[n:, :LANES])
    den = jnp.where(head0, o[:n, LANES:], o[n:, LANES:])
    mx = jnp.where(head0, jnp.broadcast_to(m[:n], (n, LANES)),
                   jnp.broadcast_to(m[n:], (n, LANES)))
    return mx, den, acc


def _attn_kernel(*refs):
    nb = len(DILATIONS)
    q_refs, k_refs, kh_refs, v_refs, vh_refs = (refs[i * nb:(i + 1) * nb] for i in range(5))
    bias_ref, y_ref = refs[5 * nb], refs[5 * nb + 1]
    stat_refs = refs[5 * nb + 2:]
    n = ATTN_BLOCK
    halo_slab = jnp.where(pl.program_id(2) > 0, 0, 1)

    def block_stats(bi, j, r):
        q_ref, k_ref, kh_ref, v_ref, vh_ref = (
            q_refs[bi], k_refs[bi], kh_refs[bi], v_refs[bi], vh_refs[bi])
        cols = slice(r * LANES, (r + 1) * LANES)
        cur = slice(j * n, (j + 1) * n)
        if j == 0:
            kp, vp, bias = kh_ref[:, cols], vh_ref[:, cols], bias_ref[halo_slab]
        else:
            prev = slice((j - 1) * n, j * n)
            kp, vp, bias = k_ref[prev, cols], v_ref[prev, cols], bias_ref[0]
        return _attn_unit(q_ref[cur, cols], kp, k_ref[cur, cols], vp, v_ref[cur, cols], bias)

    for bi in range(nb - 1, 0, -1):
        d = DILATIONS[bi]
        for r in range(d):
            for j in range(ATTN_TILE // (d * n)):
                rows = pl.ds(j * n * d + r, n, stride=d)
                for ref, val in zip(stat_refs[3 * (bi - 1):3 * bi], block_stats(bi, j, r)):
                    ref[rows, :] = val

    for j in range(ATTN_TILE // n):
        rows = slice(j * n, (j + 1) * n)
        stats = [block_stats(0, j, 0)] + [
            tuple(ref[rows, :] for ref in stat_refs[3 * (bi - 1):3 * bi]) for bi in range(1, nb)]
        m = functools.reduce(jnp.maximum, [st[0] for st in stats])
        ws = [jnp.exp2(st[0] - m) for st in stats]
        den = sum(w * st[1] for w, st in zip(ws, stats))
        num = sum(w * st[2] for w, st in zip(ws, stats))
        y_ref[rows, :] = (num / den).astype(y_ref.dtype)


def _dilated_attention(qs, ks, vs):
    b, hp, s, _ = qs[0].shape
    n = ATTN_BLOCK

    def tile_spec(d):
        return pl.BlockSpec((None, None, ATTN_TILE // d, d * LANES),
                            lambda i, h, t: (i, h, t, 0))

    def halo_spec(d):
        per_tile = ATTN_TILE // (d * n)
        return pl.BlockSpec((None, None, n, d * LANES),
                            lambda i, h, t: (i, h, jnp.maximum(t * per_tile - 1, 0), 0))

    tiles = [tile_spec(d) for d in DILATIONS]
    halos = [halo_spec(d) for d in DILATIONS]
    args = list(qs) + list(ks) * 2 + list(vs) * 2 + [_band_bias()]
    return pl.pallas_call(
        _attn_kernel,
        grid=(b, hp, s // ATTN_TILE),
        in_specs=tiles + tiles + halos + tiles + halos + [_resident((2, n, 2 * n))],
        out_specs=pl.BlockSpec((None, ATTN_TILE, LANES), lambda i, h, t: (i, t, h)),
        out_shape=jax.ShapeDtypeStruct((b, s, hp * LANES), _BF16),
        scratch_shapes=[pltpu.VMEM((ATTN_TILE, LANES), _F32)] * (3 * (len(DILATIONS) - 1)),
        compiler_params=_params(3),
        name="dilated_attention",
    )(*args)


def kernel(x, c, w_ada, b_ada, norm_g, w_in, q_norm_g, k_norm_g, conv_w, conv_b,
           w_out, ffn_w1, ffn_w2):
    b, s, d = x.shape
    depth = w_ada.shape[0]
    a = conv_w.shape[-1]
    n_heads = a // HEAD_DIM
    assert d - a == a and a % LANES == 0 and s % ATTN_TILE == 0 and DILATIONS[0] == 1
    assert w_ada.shape[-1] == N_SUB * N_MOD * d

    rows = -(-b // SUBLANES) * SUBLANES
    c_pad = jnp.pad(c, ((0, rows - b), (0, 0)))
    mod = _modulation(c_pad, w_ada, b_ada)[:, :b].reshape(depth, b, N_SUB, N_MOD, 1, d)

    head_id = jnp.arange(MXU_DIM) // HEAD_DIM
    pool = jnp.where(head_id[:, None] == head_id[None, :], 1.0 / HEAD_DIM, 0.0).astype(_BF16)

    w1 = ffn_w1.astype(_BF16)
    w2 = ffn_w2.astype(_BF16)
    w_in_b = w_in.astype(_BF16)
    w_out_b = w_out.astype(_BF16)

    for layer in range(depth):
        shift = lambda i: mod[layer, :, i, 0]
        scale = lambda i: mod[layer, :, i, 1]
        gate = lambda i: mod[layer, :, i, 2]
        g = lambda i: norm_g[layer, i][None, :]

        x = _ffn(x, g(0), shift(0), scale(0), gate(0), w1, w2, (layer, 0))
        qs, ks, vs, yc = _mixer_proj(
            x, g(1), shift(1), scale(1), w_in_b, (layer,), pool,
            jnp.tile(q_norm_g[layer], n_heads)[None, :],
            jnp.tile(k_norm_g[layer], n_heads)[None, :],
            conv_w[layer], conv_b[layer][None, :])
        ya = _dilated_attention(qs, ks, vs)
        x = _ffn(x, g(2), shift(2), scale(2), gate(2), w1, w2, (layer, 1),
                 mixer=(ya, yc, gate(1), w_out_b, (layer,)))
    return x
```

```python
import functools

import jax
import jax.numpy as jnp
from jax import lax
from jax.experimental import pallas as pl
from jax.experimental.pallas import tpu as pltpu

EPS = 1e-6
NEG = -1e30
LOG2_E = 1.4426950408889634
MAX_UNSHIFTED_SCORE = 60.0
HEAD_DIM = 64
N_SUB = 3
N_MOD = 3
CONV_K = 3
DILATIONS = (1, 4, 16)
ATTN_BLOCK = 128
LANES = 128
SUBLANES = 8
MXU_DIM = 256
ATTN_TILE = ATTN_BLOCK * DILATIONS[-1]
FFN_ROW_GROUPS = 4
VMEM_LIMIT_BYTES = 56 * 1024 * 1024

_BF16 = jnp.bfloat16
_F32 = jnp.float32


def _params(n_axes):
    return pltpu.CompilerParams(
        dimension_semantics=("arbitrary",) * n_axes,
        vmem_limit_bytes=VMEM_LIMIT_BYTES)


def _resident(shape, lead=()):
    nd = len(shape) - len(lead)
    return pl.BlockSpec((None,) * len(lead) + tuple(shape[len(lead):]),
                        lambda *_: tuple(lead) + (0,) * nd,
                        pipeline_mode=pl.Buffered(1))


def _silu(z):
    return z * (1.0 / (1.0 + jnp.exp(-z)))


def _ada_norm(x, g, shift, scale):
    y = x * lax.rsqrt(jnp.mean(x * x, axis=-1, keepdims=True) + EPS)
    return (y * g) * (1.0 + scale) + shift


def _mod_kernel(c_ref, w_ref, b_ref, o_ref):
    a = _silu(c_ref[...]).astype(_BF16)
    o_ref[0] = jnp.dot(a, w_ref[0].astype(_BF16),
                       preferred_element_type=_F32) + b_ref[0]


def _modulation(c_pad, w_ada, b_ada):
    depth, d, n = w_ada.shape
    rows = c_pad.shape[0]
    tn = 1024
    return pl.pallas_call(
        _mod_kernel,
        grid=(depth, n // tn),
        in_specs=[
            pl.BlockSpec((rows, d), lambda l, j: (0, 0)),
            pl.BlockSpec((1, d, tn), lambda l, j: (l, 0, j)),
            pl.BlockSpec((1, 1, tn), lambda l, j: (l, 0, j)),
        ],
        out_specs=pl.BlockSpec((1, rows, tn), lambda l, j: (l, 0, j)),
        out_shape=jax.ShapeDtypeStruct((depth, rows, n), _F32),
        compiler_params=_params(2),
        name="ada_modulation",
    )(c_pad, w_ada, b_ada.reshape(depth, 1, n))


def _ffn_kernel(x_ref, g_ref, shift_ref, scale_ref, gate_ref, w1_ref, w2_ref, *refs):
    o_ref = refs[-1]
    tm = x_ref.shape[1]
    f = w2_ref.shape[0]
    for lo in range(0, tm, tm // FFN_ROW_GROUPS):
        rows = slice(lo, lo + tm // FFN_ROW_GROUPS)
        x = x_ref[0, rows]
        if len(refs) > 1:
            ya_ref, yc_ref, gate_mix_ref, wo_ref = refs[:-1]
            a = ya_ref.shape[2]
            mix = (jnp.dot(ya_ref[0, rows], wo_ref[0:a, :], preferred_element_type=_F32)
                   + jnp.dot(yc_ref[0, rows], wo_ref[a:, :], preferred_element_type=_F32))
            x = x + gate_mix_ref[0] * mix
        h = _ada_norm(x, g_ref[...], shift_ref[0], scale_ref[0]).astype(_BF16)
        gu = jnp.dot(h, w1_ref[...], preferred_element_type=_F32)
        act = (_silu(gu[:, :f]) * gu[:, f:]).astype(_BF16)
        y = jnp.dot(act, w2_ref[...], preferred_element_type=_F32)
        o_ref[0, rows] = x + (0.5 * gate_ref[0]) * y


def _ffn(x, g, shift, scale, gate, w1, w2, lead, mixer=None, tm=1024):
    b, s, d = x.shape
    row = pl.BlockSpec((1, tm, d), lambda i, j: (i, j, 0))
    vec = pl.BlockSpec((1, 1, d), lambda i, j: (i, 0, 0))
    in_specs = [row, pl.BlockSpec((1, d), lambda i, j: (0, 0)), vec, vec, vec,
                _resident(w1.shape, lead), _resident(w2.shape, lead)]
    args = [x, g, shift, scale, gate, w1, w2]
    if mixer is not None:
        ya, yc, gate_mix, w_out, w_out_lead = mixer
        half = pl.BlockSpec((1, tm, ya.shape[2]), lambda i, j: (i, j, 0))
        in_specs += [half, half, vec, _resident(w_out.shape, w_out_lead)]
        args += [ya, yc, gate_mix, w_out]
    return pl.pallas_call(
        _ffn_kernel,
        grid=(b, s // tm),
        in_specs=in_specs,
        out_specs=row,
        out_shape=jax.ShapeDtypeStruct(x.shape, x.dtype),
        compiler_params=_params(2),
        name="swiglu_half_step",
    )(*args)


def _head_rmsnorm(z, gain, pool):
    zz = z * z
    hi = zz.astype(_BF16)
    lo = (zz - hi.astype(_F32)).astype(_BF16)
    width = pool.shape[0]
    ms = jnp.concatenate(
        [jnp.dot(hi[:, c:c + width], pool, preferred_element_type=_F32)
         + jnp.dot(lo[:, c:c + width], pool, preferred_element_type=_F32)
         for c in range(0, z.shape[1], width)], axis=1)
    return (z * lax.rsqrt(ms + EPS)) * gain


def _store_dilated(z, out_refs, stage_refs):
    tm, a = z.shape
    n_hp = a // LANES
    for hp in range(n_hp):
        zc = z[:, hp * LANES:(hp + 1) * LANES]
        out_refs[0][0, hp] = zc.astype(_BF16)
        stage_refs[0][hp * tm:(hp + 1) * tm, :] = zc
    for i in range(1, len(DILATIONS)):
        d_prev, d = DILATIONS[i - 1], DILATIONS[i]
        step = d // d_prev
        rows_prev, rows = tm // d_prev, tm // d
        for hp in range(n_hp):
            for r_prev in range(d_prev):
                base = (hp * d_prev + r_prev) * rows_prev
                for sub in range(step):
                    r = r_prev + d_prev * sub
                    piece = stage_refs[i - 1][pl.ds(base + sub, rows, stride=step), :]
                    out_refs[i][0, hp, :, r * LANES:(r + 1) * LANES] = piece.astype(_BF16)
                    if i + 1 < len(DILATIONS):
                        dst = (hp * d + r) * rows
                        stage_refs[i][dst:dst + rows, :] = piece


def _proj_kernel(x_ref, g_ref, shift_ref, scale_ref, w_in_ref, pool_ref, qg_ref,
                 kg_ref, cw_ref, cb_ref, *refs):
    nb = len(DILATIONS)
    q_refs, k_refs, v_refs = refs[0:nb], refs[nb:2 * nb], refs[2 * nb:3 * nb]
    yc_ref, cv_ref = refs[3 * nb], refs[3 * nb + 1]
    stage_refs = refs[3 * nb + 2:]
    tm = x_ref.shape[1]
    a = yc_ref.shape[2]
    halo = SUBLANES

    @pl.when(pl.program_id(1) == 0)
    def _():
        cv_ref[...] = jnp.zeros((halo, a), _F32)

    h = _ada_norm(x_ref[0], g_ref[...], shift_ref[0], scale_ref[0]).astype(_BF16)

    def proj(lo, hi):
        return jnp.dot(h, w_in_ref[:, lo * a:hi * a], preferred_element_type=_F32)

    conv_in = proj(3, 6)
    gate_b = conv_in[:, 0:a]
    cv = conv_in[:, a:2 * a] * conv_in[:, 2 * a:3 * a]
    ext = jnp.concatenate([cv_ref[...], cv], axis=0)
    conv = cw_ref[CONV_K - 1:CONV_K, :] * cv
    for back in range(1, CONV_K):
        tap = cw_ref[CONV_K - 1 - back:CONV_K - back, :]
        conv = conv + tap * pltpu.roll(ext, back, axis=0)[halo:]
    yc_ref[0] = (gate_b * (conv + cb_ref[...])).astype(_BF16)
    cv_ref[...] = cv[tm - halo:]

    pool = pool_ref[...]
    n_stage = nb - 1
    q = _head_rmsnorm(proj(0, 1), qg_ref[...], pool) * (HEAD_DIM ** -0.5 * LOG2_E)
    _store_dilated(q, q_refs, stage_refs[0:n_stage])
    k = _head_rmsnorm(proj(1, 2), kg_ref[...], pool)
    _store_dilated(k, k_refs, stage_refs[n_stage:2 * n_stage])
    _store_dilated(proj(2, 3), v_refs, stage_refs[2 * n_stage:3 * n_stage])


def _mixer_proj(x, g, shift, scale, w_in, lead, pool, qg, kg, conv_w, conv_b, tm=512):
    b, s, d = x.shape
    a = conv_w.shape[-1]
    row = pl.BlockSpec((1, tm, d), lambda i, j: (i, j, 0))
    vec = pl.BlockSpec((1, 1, d), lambda i, j: (i, 0, 0))
    small = lambda shape: pl.BlockSpec(shape, lambda i, j: (0, 0))
    n_hp = a // LANES
    heads = [pl.BlockSpec((1, n_hp, tm // dl, dl * LANES), lambda i, j: (i, 0, j, 0))
             for dl in DILATIONS]
    qkv_shapes = [jax.ShapeDtypeStruct((b, n_hp, s // dl, dl * LANES), _BF16)
                  for dl in DILATIONS]
    outs = pl.pallas_call(
        _proj_kernel,
        grid=(b, s // tm),
        in_specs=[row, small((1, d)), vec, vec, _resident(w_in.shape, lead),
                  _resident(pool.shape), small((1, a)), small((1, a)),
                  small((CONV_K, a)), small((1, a))],
        out_specs=heads * 3 + [pl.BlockSpec((1, tm, a), lambda i, j: (i, j, 0))],
        out_shape=qkv_shapes * 3 + [jax.ShapeDtypeStruct((b, s, a), _BF16)],
        scratch_shapes=([pltpu.VMEM((SUBLANES, a), _F32)]
                        + [pltpu.VMEM((n_hp * tm, LANES), _F32)] * (3 * (len(DILATIONS) - 1))),
        compiler_params=_params(2),
        name="mixer_in_proj",
    )(x, g, shift, scale, w_in, pool, qg, kg, conv_w, conv_b)
    nb = len(DILATIONS)
    return outs[0:nb], outs[nb:2 * nb], outs[2 * nb:3 * nb], outs[3 * nb]


def _band_bias():
    n = ATTN_BLOCK
    qi = jnp.arange(n)[:, None]
    kj = jnp.arange(2 * n)[None, :]
    dist = qi + n - kj
    band = (dist >= 0) & (dist <= n)
    masks = jnp.stack([band, band & (kj >= n)])
    return jnp.where(masks, 0.0, NEG).astype(_F32)


def _attn_unit(qb, kprev, kcur, vprev, vcur, bias, row_max):
    n = ATTN_BLOCK
    lane = lax.broadcasted_iota(jnp.int32, (n, LANES), 1)
    head0 = lane < HEAD_DIM
    zero = jnp.zeros_like(qb)
    q2 = jnp.concatenate([jnp.where(head0, qb, zero), jnp.where(head0, zero, qb)], axis=0)
    kk = jnp.concatenate([kprev, kcur], axis=0)
    s = lax.dot_general(q2, kk, (((1,), (1,)), ((), ())),
                        preferred_element_type=_F32)
    s = jnp.concatenate([s[:n] + bias, s[n:] + bias], axis=0)
    if row_max:
        m = jnp.max(s, axis=-1, keepdims=True)
        s = s - m
    p = jnp.exp2(s).astype(_BF16)
    vv = jnp.concatenate([vprev, vcur], axis=0)
    vext = jnp.concatenate([vv, jnp.ones_like(vv)], axis=1)
    o = jnp.dot(p, vext, preferred_element_type=_F32)
    stats = [jnp.where(head0, o[:n, LANES:], o[n:, LANES:]),
             jnp.where(head0, o[:n, :LANES], o[n:, :LANES])]
    if row_max:
        stats.insert(0, jnp.where(head0, jnp.broadcast_to(m[:n], (n, LANES)),
                                  jnp.broadcast_to(m[n:], (n, LANES))))
    return stats


def _attn_kernel(*refs, row_max):
    nb = len(DILATIONS)
    q_refs, k_refs, kh_refs, v_refs, vh_refs = (refs[i * nb:(i + 1) * nb] for i in range(5))
    bias_ref, y_ref = refs[5 * nb], refs[5 * nb + 1]
    stat_refs = refs[5 * nb + 2:]
    n_stat = len(stat_refs) // (nb - 1)
    n = ATTN_BLOCK
    halo_slab = jnp.where(pl.program_id(2) > 0, 0, 1)

    def block_stats(bi, j, r):
        q_ref, k_ref, kh_ref, v_ref, vh_ref = (
            q_refs[bi], k_refs[bi], kh_refs[bi], v_refs[bi], vh_refs[bi])
        cols = slice(r * LANES, (r + 1) * LANES)
        cur = slice(j * n, (j + 1) * n)
        if j == 0:
            kp, vp, bias = kh_ref[:, cols], vh_ref[:, cols], bias_ref[halo_slab]
        else:
            prev = slice((j - 1) * n, j * n)
            kp, vp, bias = k_ref[prev, cols], v_ref[prev, cols], bias_ref[0]
        return _attn_unit(q_ref[cur, cols], kp, k_ref[cur, cols], vp, v_ref[cur, cols],
                          bias, row_max)

    for bi in range(nb - 1, 0, -1):
        d = DILATIONS[bi]
        for r in range(d):
            for j in range(ATTN_TILE // (d * n)):
                rows = pl.ds(j * n * d + r, n, stride=d)
                for ref, val in zip(stat_refs[n_stat * (bi - 1):n_stat * bi],
                                    block_stats(bi, j, r)):
                    ref[rows, :] = val

    for j in range(ATTN_TILE // n):
        rows = slice(j * n, (j + 1) * n)
        stats = [block_stats(0, j, 0)] + [
            [ref[rows, :] for ref in stat_refs[n_stat * (bi - 1):n_stat * bi]]
            for bi in range(1, nb)]
        if row_max:
            m = functools.reduce(jnp.maximum, [st[0] for st in stats])
            ws = [jnp.exp2(st[0] - m) for st in stats]
            den = sum(w * st[1] for w, st in zip(ws, stats))
            num = sum(w * st[2] for w, st in zip(ws, stats))
        else:
            den = sum(st[0] for st in stats)
            num = sum(st[1] for st in stats)
        y_ref[rows, :] = (num / den).astype(y_ref.dtype)


def _dilated_attention(qs, ks, vs, row_max):
    b, hp, s, _ = qs[0].shape
    n = ATTN_BLOCK

    def tile_spec(d):
        return pl.BlockSpec((None, None, ATTN_TILE // d, d * LANES),
                            lambda i, h, t: (i, h, t, 0))

    def halo_spec(d):
        per_tile = ATTN_TILE // (d * n)
        return pl.BlockSpec((None, None, n, d * LANES),
                            lambda i, h, t: (i, h, jnp.maximum(t * per_tile - 1, 0), 0))

    tiles = [tile_spec(d) for d in DILATIONS]
    halos = [halo_spec(d) for d in DILATIONS]
    args = list(qs) + list(ks) * 2 + list(vs) * 2 + [_band_bias()]
    return pl.pallas_call(
        functools.partial(_attn_kernel, row_max=row_max),
        grid=(b, hp, s // ATTN_TILE),
        in_specs=tiles + tiles + halos + tiles + halos + [_resident((2, n, 2 * n))],
        out_specs=pl.BlockSpec((None, ATTN_TILE, LANES), lambda i, h, t: (i, t, h)),
        out_shape=jax.ShapeDtypeStruct((b, s, hp * LANES), _BF16),
        scratch_shapes=([pltpu.VMEM((ATTN_TILE, LANES), _F32)]
                        * ((3 if row_max else 2) * (len(DILATIONS) - 1))),
        compiler_params=_params(3),
        name="dilated_attention",
    )(*args)


def kernel(x, c, w_ada, b_ada, norm_g, w_in, q_norm_g, k_norm_g, conv_w, conv_b,
           w_out, ffn_w1, ffn_w2):
    b, s, d = x.shape
    depth = w_ada.shape[0]
    a = conv_w.shape[-1]
    n_heads = a // HEAD_DIM
    assert d - a == a and a % LANES == 0 and s % ATTN_TILE == 0 and DILATIONS[0] == 1
    assert w_ada.shape[-1] == N_SUB * N_MOD * d

    rows = -(-b // SUBLANES) * SUBLANES
    c_pad = jnp.pad(c, ((0, rows - b), (0, 0)))
    mod = _modulation(c_pad, w_ada, b_ada)[:, :b].reshape(depth, b, N_SUB, N_MOD, 1, d)

    head_id = jnp.arange(MXU_DIM) // HEAD_DIM
    pool = jnp.where(head_id[:, None] == head_id[None, :], 1.0 / HEAD_DIM, 0.0).astype(_BF16)

    w1 = ffn_w1.astype(_BF16)
    w2 = ffn_w2.astype(_BF16)
    w_in_b = w_in.astype(_BF16)
    w_out_b = w_out.astype(_BF16)

    for layer in range(depth):
        shift = lambda i: mod[layer, :, i, 0]
        scale = lambda i: mod[layer, :, i, 1]
        gate = lambda i: mod[layer, :, i, 2]
        g = lambda i: norm_g[layer, i][None, :]

        x = _ffn(x, g(0), shift(0), scale(0), gate(0), w1, w2, (layer, 0))
        qs, ks, vs, yc = _mixer_proj(
            x, g(1), shift(1), scale(1), w_in_b, (layer,), pool,
            jnp.tile(q_norm_g[layer], n_heads)[None, :],
            jnp.tile(k_norm_g[layer], n_heads)[None, :],
            conv_w[layer], conv_b[layer][None, :])
        score_bound = (HEAD_DIM ** 0.5 * LOG2_E * jnp.max(jnp.abs(q_norm_g[layer]))
                       * jnp.max(jnp.abs(k_norm_g[layer])))
        ya = lax.cond(score_bound <= MAX_UNSHIFTED_SCORE,
                      functools.partial(_dilated_attention, row_max=False),
                      functools.partial(_dilated_attention, row_max=True),
                      qs, ks, vs)
        x = _ffn(x, g(2), shift(2), scale(2), gate(2), w1, w2, (layer, 1),
                 mixer=(ya, yc, gate(1), w_out_b, (layer,)))
    return x
```

```python
import functools

import jax
import jax.numpy as jnp
from jax import lax
from jax.experimental import pallas as pl
from jax.experimental.pallas import tpu as pltpu

EPS = 1e-6
NEG = -1e30
LOG2_E = 1.4426950408889634
MAX_UNSHIFTED_SCORE = 60.0
HEAD_DIM = 64
N_SUB = 3
N_MOD = 3
CONV_K = 3
DILATIONS = (1, 4, 16)
ATTN_BLOCK = 128
LANES = 128
SUBLANES = 8
ATTN_TILE = ATTN_BLOCK * DILATIONS[-1]
FFN_ROW_GROUPS = 4
VMEM_LIMIT_BYTES = 56 * 1024 * 1024

_BF16 = jnp.bfloat16
_F32 = jnp.float32


def _params(n_axes):
    return pltpu.CompilerParams(
        dimension_semantics=("arbitrary",) * n_axes,
        vmem_limit_bytes=VMEM_LIMIT_BYTES)


def _resident(shape, lead=()):
    nd = len(shape) - len(lead)
    return pl.BlockSpec((None,) * len(lead) + tuple(shape[len(lead):]),
                        lambda *_: tuple(lead) + (0,) * nd,
                        pipeline_mode=pl.Buffered(1))


def _silu(z):
    return z * (1.0 / (1.0 + jnp.exp(-z)))


def _ada_norm(x, g, shift, scale):
    y = x * lax.rsqrt(jnp.mean(x * x, axis=-1, keepdims=True) + EPS)
    return (y * g) * (1.0 + scale) + shift


def _mod_kernel(c_ref, w_ref, b_ref, o_ref):
    a = _silu(c_ref[...]).astype(_BF16)
    o_ref[0] = jnp.dot(a, w_ref[0].astype(_BF16),
                       preferred_element_type=_F32) + b_ref[0]


def _modulation(c_pad, w_ada, b_ada):
    depth, d, n = w_ada.shape
    rows = c_pad.shape[0]
    tn = 1024
    return pl.pallas_call(
        _mod_kernel,
        grid=(depth, n // tn),
        in_specs=[
            pl.BlockSpec((rows, d), lambda l, j: (0, 0)),
            pl.BlockSpec((1, d, tn), lambda l, j: (l, 0, j)),
            pl.BlockSpec((1, 1, tn), lambda l, j: (l, 0, j)),
        ],
        out_specs=pl.BlockSpec((1, rows, tn), lambda l, j: (l, 0, j)),
        out_shape=jax.ShapeDtypeStruct((depth, rows, n), _F32),
        compiler_params=_params(2),
        name="ada_modulation",
    )(c_pad, w_ada, b_ada.reshape(depth, 1, n))


def _ffn_kernel(x_ref, g_ref, shift_ref, scale_ref, gate_ref, w1_ref, w2_ref, *refs):
    o_ref = refs[-1]
    tm = x_ref.shape[1]
    f = w2_ref.shape[0]
    for lo in range(0, tm, tm // FFN_ROW_GROUPS):
        rows = slice(lo, lo + tm // FFN_ROW_GROUPS)
        x = x_ref[0, rows]
        if len(refs) > 1:
            ya_ref, yc_ref, gate_mix_ref, wo_ref = refs[:-1]
            a = ya_ref.shape[2]
            mix = (jnp.dot(ya_ref[0, rows], wo_ref[0:a, :], preferred_element_type=_F32)
                   + jnp.dot(yc_ref[0, rows], wo_ref[a:, :], preferred_element_type=_F32))
            x = x + gate_mix_ref[0] * mix
        h = _ada_norm(x, g_ref[...], shift_ref[0], scale_ref[0]).astype(_BF16)
        gu = jnp.dot(h, w1_ref[...], preferred_element_type=_F32)
        act = (_silu(gu[:, :f]) * gu[:, f:]).astype(_BF16)
        y = jnp.dot(act, w2_ref[...], preferred_element_type=_F32)
        o_ref[0, rows] = x + (0.5 * gate_ref[0]) * y


def _ffn(x, g, shift, scale, gate, w1, w2, lead, mixer=None, tm=1024):
    b, s, d = x.shape
    row = pl.BlockSpec((1, tm, d), lambda i, j: (i, j, 0))
    vec = pl.BlockSpec((1, 1, d), lambda i, j: (i, 0, 0))
    in_specs = [row, pl.BlockSpec((1, d), lambda i, j: (0, 0)), vec, vec, vec,
                _resident(w1.shape, lead), _resident(w2.shape, lead)]
    args = [x, g, shift, scale, gate, w1, w2]
    if mixer is not None:
        ya, yc, gate_mix, w_out, w_out_lead = mixer
        half = pl.BlockSpec((1, tm, ya.shape[2]), lambda i, j: (i, j, 0))
        in_specs += [half, half, vec, _resident(w_out.shape, w_out_lead)]
        args += [ya, yc, gate_mix, w_out]
    return pl.pallas_call(
        _ffn_kernel,
        grid=(b, s // tm),
        in_specs=in_specs,
        out_specs=row,
        out_shape=jax.ShapeDtypeStruct(x.shape, x.dtype),
        compiler_params=_params(2),
        name="swiglu_half_step",
    )(*args)


def _head_rmsnorm(z, gain):
    rows, a = z.shape
    first = lax.broadcasted_iota(jnp.int32, (rows, LANES), 1) < HEAD_DIM
    out = []
    for c in range(0, a, LANES):
        zc = z[:, c:c + LANES]
        zz = zc * zc
        s0 = jnp.sum(jnp.where(first, zz, 0.0), axis=-1, keepdims=True)
        s1 = jnp.sum(jnp.where(first, 0.0, zz), axis=-1, keepdims=True)
        ms = jnp.where(first, s0, s1) * (1.0 / HEAD_DIM)
        out.append(zc * lax.rsqrt(ms + EPS))
    return jnp.concatenate(out, axis=1) * gain


def _store_dilated(z, out_refs, stage_refs):
    tm, a = z.shape
    n_hp = a // LANES
    for hp in range(n_hp):
        zc = z[:, hp * LANES:(hp + 1) * LANES]
        out_refs[0][0, hp] = zc.astype(_BF16)
        stage_refs[0][hp * tm:(hp + 1) * tm, :] = zc
    for i in range(1, len(DILATIONS)):
        d_prev, d = DILATIONS[i - 1], DILATIONS[i]
        step = d // d_prev
        rows_prev, rows = tm // d_prev, tm // d
        for hp in range(n_hp):
            for r_prev in range(d_prev):
                base = (hp * d_prev + r_prev) * rows_prev
                for sub in range(step):
                    r = r_prev + d_prev * sub
                    piece = stage_refs[i - 1][pl.ds(base + sub, rows, stride=step), :]
                    out_refs[i][0, hp, :, r * LANES:(r + 1) * LANES] = piece.astype(_BF16)
                    if i + 1 < len(DILATIONS):
                        dst = (hp * d + r) * rows
                        stage_refs[i][dst:dst + rows, :] = piece


def _proj_kernel(x_ref, g_ref, shift_ref, scale_ref, w_in_ref, qg_ref, kg_ref,
                 cw_ref, cb_ref, *refs):
    nb = len(DILATIONS)
    q_refs, k_refs, v_refs = refs[0:nb], refs[nb:2 * nb], refs[2 * nb:3 * nb]
    yc_ref, cv_ref = refs[3 * nb], refs[3 * nb + 1]
    stage_refs = refs[3 * nb + 2:]
    tm = x_ref.shape[1]
    a = yc_ref.shape[2]
    halo = SUBLANES

    @pl.when(pl.program_id(1) == 0)
    def _():
        cv_ref[...] = jnp.zeros((halo, a), _F32)

    h = _ada_norm(x_ref[0], g_ref[...], shift_ref[0], scale_ref[0]).astype(_BF16)

    def proj(lo, hi):
        return jnp.dot(h, w_in_ref[:, lo * a:hi * a], preferred_element_type=_F32)

    conv_in = proj(3, 6)
    gate_b = conv_in[:, 0:a]
    cv = conv_in[:, a:2 * a] * conv_in[:, 2 * a:3 * a]
    ext = jnp.concatenate([cv_ref[...], cv], axis=0)
    conv = cw_ref[CONV_K - 1:CONV_K, :] * cv
    for back in range(1, CONV_K):
        tap = cw_ref[CONV_K - 1 - back:CONV_K - back, :]
        conv = conv + tap * pltpu.roll(ext, back, axis=0)[halo:]
    yc_ref[0] = (gate_b * (conv + cb_ref[...])).astype(_BF16)
    cv_ref[...] = cv[tm - halo:]

    n_stage = nb - 1
    q = _head_rmsnorm(proj(0, 1), qg_ref[...]) * (HEAD_DIM ** -0.5 * LOG2_E)
    _store_dilated(q, q_refs, stage_refs[0:n_stage])
    k = _head_rmsnorm(proj(1, 2), kg_ref[...])
    _store_dilated(k, k_refs, stage_refs[n_stage:2 * n_stage])
    _store_dilated(proj(2, 3), v_refs, stage_refs[2 * n_stage:3 * n_stage])


def _mixer_proj(x, g, shift, scale, w_in, lead, qg, kg, conv_w, conv_b, tm=1024):
    b, s, d = x.shape
    a = conv_w.shape[-1]
    row = pl.BlockSpec((1, tm, d), lambda i, j: (i, j, 0))
    vec = pl.BlockSpec((1, 1, d), lambda i, j: (i, 0, 0))
    small = lambda shape: pl.BlockSpec(shape, lambda i, j: (0, 0))
    n_hp = a // LANES
    heads = [pl.BlockSpec((1, n_hp, tm // dl, dl * LANES), lambda i, j: (i, 0, j, 0))
             for dl in DILATIONS]
    qkv_shapes = [jax.ShapeDtypeStruct((b, n_hp, s // dl, dl * LANES), _BF16)
                  for dl in DILATIONS]
    outs = pl.pallas_call(
        _proj_kernel,
        grid=(b, s // tm),
        in_specs=[row, small((1, d)), vec, vec, _resident(w_in.shape, lead),
                  small((1, a)), small((1, a)), small((CONV_K, a)), small((1, a))],
        out_specs=heads * 3 + [pl.BlockSpec((1, tm, a), lambda i, j: (i, j, 0))],
        out_shape=qkv_shapes * 3 + [jax.ShapeDtypeStruct((b, s, a), _BF16)],
        scratch_shapes=([pltpu.VMEM((SUBLANES, a), _F32)]
                        + [pltpu.VMEM((n_hp * tm, LANES), _F32)] * (3 * (len(DILATIONS) - 1))),
        compiler_params=_params(2),
        name="mixer_in_proj",
    )(x, g, shift, scale, w_in, qg, kg, conv_w, conv_b)
    nb = len(DILATIONS)
    return outs[0:nb], outs[nb:2 * nb], outs[2 * nb:3 * nb], outs[3 * nb]


def _band_bias():
    n = ATTN_BLOCK
    qi = jnp.arange(n)[:, None]
    kj = jnp.arange(2 * n)[None, :]
    dist = qi + n - kj
    band = (dist >= 0) & (dist <= n)
    masks = jnp.stack([band, band & (kj >= n)])
    return jnp.where(masks, 0.0, NEG).astype(_F32)


def _attn_unit(qb, kprev, kcur, vprev, vcur, bias, row_max):
    n = ATTN_BLOCK
    lane = lax.broadcasted_iota(jnp.int32, (n, LANES), 1)
    head0 = lane < HEAD_DIM
    zero = jnp.zeros_like(qb)
    q2 = jnp.concatenate([jnp.where(head0, qb, zero), jnp.where(head0, zero, qb)], axis=0)
    kk = jnp.concatenate([kprev, kcur], axis=0)
    s = lax.dot_general(q2, kk, (((1,), (1,)), ((), ())),
                        preferred_element_type=_F32)
    s = jnp.concatenate([s[:n] + bias, s[n:] + bias], axis=0)
    if row_max:
        m = jnp.max(s, axis=-1, keepdims=True)
        s = s - m
    p = jnp.exp2(s).astype(_BF16)
    vv = jnp.concatenate([vprev, vcur], axis=0)
    vext = jnp.concatenate([vv, jnp.ones_like(vv)], axis=1)
    o = jnp.dot(p, vext, preferred_element_type=_F32)
    stats = [jnp.where(head0, o[:n, LANES:], o[n:, LANES:]),
             jnp.where(head0, o[:n, :LANES], o[n:, :LANES])]
    if row_max:
        stats.insert(0, jnp.where(head0, jnp.broadcast_to(m[:n], (n, LANES)),
                                  jnp.broadcast_to(m[n:], (n, LANES))))
    return stats


def _attn_kernel(*refs, row_max):
    nb = len(DILATIONS)
    q_refs, k_refs, kh_refs, v_refs, vh_refs = (refs[i * nb:(i + 1) * nb] for i in range(5))
    bias_ref, y_ref = refs[5 * nb], refs[5 * nb + 1]
    stat_refs = refs[5 * nb + 2:]
    n_stat = len(stat_refs) // (nb - 1)
    n = ATTN_BLOCK
    halo_slab = jnp.where(pl.program_id(2) > 0, 0, 1)

    def block_stats(bi, j, r):
        q_ref, k_ref, kh_ref, v_ref, vh_ref = (
            q_refs[bi], k_refs[bi], kh_refs[bi], v_refs[bi], vh_refs[bi])
        cols = slice(r * LANES, (r + 1) * LANES)
        cur = slice(j * n, (j + 1) * n)
        if j == 0:
            kp, vp, bias = kh_ref[:, cols], vh_ref[:, cols], bias_ref[halo_slab]
        else:
            prev = slice((j - 1) * n, j * n)
            kp, vp, bias = k_ref[prev, cols], v_ref[prev, cols], bias_ref[0]
        return _attn_unit(q_ref[cur, cols], kp, k_ref[cur, cols], vp, v_ref[cur, cols],
                          bias, row_max)

    for bi in range(nb - 1, 0, -1):
        d = DILATIONS[bi]
        for r in range(d):
            for j in range(ATTN_TILE // (d * n)):
                rows = pl.ds(j * n * d + r, n, stride=d)
                for ref, val in zip(stat_refs[n_stat * (bi - 1):n_stat * bi],
                                    block_stats(bi, j, r)):
                    ref[rows, :] = val

    for j in range(ATTN_TILE // n):
        rows = slice(j * n, (j + 1) * n)
        stats = [block_stats(0, j, 0)] + [
            [ref[rows, :] for ref in stat_refs[n_stat * (bi - 1):n_stat * bi]]
            for bi in range(1, nb)]
        if row_max:
            m = functools.reduce(jnp.maximum, [st[0] for st in stats])
            ws = [jnp.exp2(st[0] - m) for st in stats]
            den = sum(w * st[1] for w, st in zip(ws, stats))
            num = sum(w * st[2] for w, st in zip(ws, stats))
        else:
            den = sum(st[0] for st in stats)
            num = sum(st[1] for st in stats)
        y_ref[rows, :] = (num / den).astype(y_ref.dtype)


def _dilated_attention(qs, ks, vs, row_max):
    b, hp, s, _ = qs[0].shape
    n = ATTN_BLOCK

    def tile_spec(d):
        return pl.BlockSpec((None, None, ATTN_TILE // d, d * LANES),
                            lambda i, h, t: (i, h, t, 0))

    def halo_spec(d):
        per_tile = ATTN_TILE // (d * n)
        return pl.BlockSpec((None, None, n, d * LANES),
                            lambda i, h, t: (i, h, jnp.maximum(t * per_tile - 1, 0), 0))

    tiles = [tile_spec(d) for d in DILATIONS]
    halos = [halo_spec(d) for d in DILATIONS]
    args = list(qs) + list(ks) * 2 + list(vs) * 2 + [_band_bias()]
    return pl.pallas_call(
        functools.partial(_attn_kernel, row_max=row_max),
        grid=(b, hp, s // ATTN_TILE),
        in_specs=tiles + tiles + halos + tiles + halos + [_resident((2, n, 2 * n))],
        out_specs=pl.BlockSpec((None, ATTN_TILE, LANES), lambda i, h, t: (i, t, h)),
        out_shape=jax.ShapeDtypeStruct((b, s, hp * LANES), _BF16),
        scratch_shapes=([pltpu.VMEM((ATTN_TILE, LANES), _F32)]
                        * ((3 if row_max else 2) * (len(DILATIONS) - 1))),
        compiler_params=_params(3),
        name="dilated_attention",
    )(*args)


def kernel(x, c, w_ada, b_ada, norm_g, w_in, q_norm_g, k_norm_g, conv_w, conv_b,
           w_out, ffn_w1, ffn_w2):
    b, s, d = x.shape
    depth = w_ada.shape[0]
    a = conv_w.shape[-1]
    n_heads = a // HEAD_DIM
    assert d - a == a and a % LANES == 0 and s % ATTN_TILE == 0 and DILATIONS[0] == 1
    assert w_ada.shape[-1] == N_SUB * N_MOD * d

    rows = -(-b // SUBLANES) * SUBLANES
    c_pad = jnp.pad(c, ((0, rows - b), (0, 0)))
    mod = _modulation(c_pad, w_ada, b_ada)[:, :b].reshape(depth, b, N_SUB, N_MOD, 1, d)

    w1 = ffn_w1.astype(_BF16)
    w2 = ffn_w2.astype(_BF16)
    w_in_b = w_in.astype(_BF16)
    w_out_b = w_out.astype(_BF16)

    for layer in range(depth):
        shift = lambda i: mod[layer, :, i, 0]
        scale = lambda i: mod[layer, :, i, 1]
        gate = lambda i: mod[layer, :, i, 2]
        g = lambda i: norm_g[layer, i][None, :]

        x = _ffn(x, g(0), shift(0), scale(0), gate(0), w1, w2, (layer, 0))
        qs, ks, vs, yc = _mixer_proj(
            x, g(1), shift(1), scale(1), w_in_b, (layer,),
            jnp.tile(q_norm_g[layer], n_heads)[None, :],
            jnp.tile(k_norm_g[layer], n_heads)[None, :],
            conv_w[layer], conv_b[layer][None, :])
        score_bound = (HEAD_DIM ** 0.5 * LOG2_E * jnp.max(jnp.abs(q_norm_g[layer]))
                       * jnp.max(jnp.abs(k_norm_g[layer])))
        ya = lax.cond(score_bound <= MAX_UNSHIFTED_SCORE,
                      functools.partial(_dilated_attention, row_max=False),
                      functools.partial(_dilated_attention, row_max=True),
                      qs, ks, vs)
        x = _ffn(x, g(2), shift(2), scale(2), gate(2), w1, w2, (layer, 1),
                 mixer=(ya, yc, gate(1), w_out_b, (layer,)))
    return x
```

```python
import functools

import jax
import jax.numpy as jnp
from jax import lax
from jax.experimental import pallas as pl
from jax.experimental.pallas import tpu as pltpu

EPS = 1e-6
NEG = -1e30
LOG2_E = 1.4426950408889634
MAX_UNSHIFTED_SCORE = 60.0
HEAD_DIM = 64
N_SUB = 3
N_MOD = 3
CONV_K = 3
DILATIONS = (1, 4, 16)
ATTN_BLOCK = 128
LANES = 128
SUBLANES = 8
ATTN_TILE = ATTN_BLOCK * DILATIONS[-1]
ATTN_HEAD_PAIRS = 2
FFN_ROW_GROUPS = 4
VMEM_LIMIT_BYTES = 56 * 1024 * 1024

_BF16 = jnp.bfloat16
_F32 = jnp.float32


def _params(n_axes):
    return pltpu.CompilerParams(
        dimension_semantics=("arbitrary",) * n_axes,
        vmem_limit_bytes=VMEM_LIMIT_BYTES)


def _resident(shape, lead=()):
    nd = len(shape) - len(lead)
    return pl.BlockSpec((None,) * len(lead) + tuple(shape[len(lead):]),
                        lambda *_: tuple(lead) + (0,) * nd,
                        pipeline_mode=pl.Buffered(1))


def _silu(z):
    return z * (1.0 / (1.0 + jnp.exp(-z)))


def _ada_norm(x, g, shift, scale):
    y = x * lax.rsqrt(jnp.mean(x * x, axis=-1, keepdims=True) + EPS)
    return (y * g) * (1.0 + scale) + shift


def _mod_kernel(c_ref, w_ref, b_ref, o_ref):
    a = _silu(c_ref[...]).astype(_BF16)
    o_ref[0] = jnp.dot(a, w_ref[0].astype(_BF16),
                       preferred_element_type=_F32) + b_ref[0]


def _modulation(c_pad, w_ada, b_ada):
    depth, d, n = w_ada.shape
    rows = c_pad.shape[0]
    tn = n // N_SUB
    return pl.pallas_call(
        _mod_kernel,
        grid=(depth, n // tn),
        in_specs=[
            pl.BlockSpec((rows, d), lambda l, j: (0, 0)),
            pl.BlockSpec((1, d, tn), lambda l, j: (l, 0, j)),
            pl.BlockSpec((1, 1, tn), lambda l, j: (l, 0, j)),
        ],
        out_specs=pl.BlockSpec((1, rows, tn), lambda l, j: (l, 0, j)),
        out_shape=jax.ShapeDtypeStruct((depth, rows, n), _F32),
        compiler_params=_params(2),
        name="ada_modulation",
    )(c_pad, w_ada, b_ada.reshape(depth, 1, n))


def _ffn_kernel(x_ref, g_ref, shift_ref, scale_ref, gate_ref, w1_ref, w2_ref, *refs):
    o_ref = refs[-1]
    tm = x_ref.shape[1]
    f = w2_ref.shape[0]
    for lo in range(0, tm, tm // FFN_ROW_GROUPS):
        rows = slice(lo, lo + tm // FFN_ROW_GROUPS)
        x = x_ref[0, rows]
        if len(refs) > 1:
            ya_ref, yc_ref, gate_mix_ref, wo_ref = refs[:-1]
            a = ya_ref.shape[2]
            mix = (jnp.dot(ya_ref[0, rows], wo_ref[0:a, :], preferred_element_type=_F32)
                   + jnp.dot(yc_ref[0, rows], wo_ref[a:, :], preferred_element_type=_F32))
            x = x + gate_mix_ref[0] * mix
        h = _ada_norm(x, g_ref[...], shift_ref[0], scale_ref[0]).astype(_BF16)
        gu = jnp.dot(h, w1_ref[...], preferred_element_type=_F32)
        act = (_silu(gu[:, :f]) * gu[:, f:]).astype(_BF16)
        y = jnp.dot(act, w2_ref[...], preferred_element_type=_F32)
        o_ref[0, rows] = x + (0.5 * gate_ref[0]) * y


def _ffn(x, g, shift, scale, gate, w1, w2, lead, mixer=None, tm=1024):
    b, s, d = x.shape
    row = pl.BlockSpec((1, tm, d), lambda i, j: (i, j, 0))
    vec = pl.BlockSpec((1, 1, d), lambda i, j: (i, 0, 0))
    in_specs = [row, pl.BlockSpec((1, d), lambda i, j: (0, 0)), vec, vec, vec,
                _resident(w1.shape, lead), _resident(w2.shape, lead)]
    args = [x, g, shift, scale, gate, w1, w2]
    if mixer is not None:
        ya, yc, gate_mix, w_out, w_out_lead = mixer
        half = pl.BlockSpec((1, tm, ya.shape[2]), lambda i, j: (i, j, 0))
        in_specs += [half, half, vec, _resident(w_out.shape, w_out_lead)]
        args += [ya, yc, gate_mix, w_out]
    return pl.pallas_call(
        _ffn_kernel,
        grid=(b, s // tm),
        in_specs=in_specs,
        out_specs=row,
        out_shape=jax.ShapeDtypeStruct(x.shape, x.dtype),
        compiler_params=_params(2),
        name="swiglu_half_step",
    )(*args)


def _head_rmsnorm(z, gain):
    rows, a = z.shape
    first = lax.broadcasted_iota(jnp.int32, (rows, LANES), 1) < HEAD_DIM
    out = []
    for c in range(0, a, LANES):
        zc = z[:, c:c + LANES]
        zz = zc * zc
        s0 = jnp.sum(jnp.where(first, zz, 0.0), axis=-1, keepdims=True)
        s1 = jnp.sum(jnp.where(first, 0.0, zz), axis=-1, keepdims=True)
        ms = jnp.where(first, s0, s1) * (1.0 / HEAD_DIM)
        out.append(zc * lax.rsqrt(ms + EPS))
    return jnp.concatenate(out, axis=1) * gain


def _store_dilated(z, out_refs, stage_refs):
    tm, a = z.shape
    n_hp = a // LANES
    for hp in range(n_hp):
        zc = z[:, hp * LANES:(hp + 1) * LANES]
        out_refs[0][0, hp] = zc.astype(_BF16)
        stage_refs[0][hp * tm:(hp + 1) * tm, :] = zc
    for i in range(1, len(DILATIONS)):
        d_prev, d = DILATIONS[i - 1], DILATIONS[i]
        step = d // d_prev
        rows_prev, rows = tm // d_prev, tm // d
        for hp in range(n_hp):
            for r_prev in range(d_prev):
                base = (hp * d_prev + r_prev) * rows_prev
                for sub in range(step):
                    r = r_prev + d_prev * sub
                    piece = stage_refs[i - 1][pl.ds(base + sub, rows, stride=step), :]
                    out_refs[i][0, hp, :, r * LANES:(r + 1) * LANES] = piece.astype(_BF16)
                    if i + 1 < len(DILATIONS):
                        dst = (hp * d + r) * rows
                        stage_refs[i][dst:dst + rows, :] = piece


def _proj_kernel(x_ref, g_ref, shift_ref, scale_ref, w_in_ref, qg_ref, kg_ref,
                 cw_ref, cb_ref, *refs):
    nb = len(DILATIONS)
    q_refs, k_refs, v_refs = refs[0:nb], refs[nb:2 * nb], refs[2 * nb:3 * nb]
    yc_ref, cv_ref = refs[3 * nb], refs[3 * nb + 1]
    stage_refs = refs[3 * nb + 2:]
    tm = x_ref.shape[1]
    a = yc_ref.shape[2]
    halo = SUBLANES

    @pl.when(pl.program_id(1) == 0)
    def _():
        cv_ref[...] = jnp.zeros((halo, a), _F32)

    h = _ada_norm(x_ref[0], g_ref[...], shift_ref[0], scale_ref[0]).astype(_BF16)

    def proj(lo, hi):
        return jnp.dot(h, w_in_ref[:, lo * a:hi * a], preferred_element_type=_F32)

    conv_in = proj(3, 6)
    gate_b = conv_in[:, 0:a]
    cv = conv_in[:, a:2 * a] * conv_in[:, 2 * a:3 * a]
    ext = jnp.concatenate([cv_ref[...], cv], axis=0)
    conv = cw_ref[CONV_K - 1:CONV_K, :] * cv
    for back in range(1, CONV_K):
        tap = cw_ref[CONV_K - 1 - back:CONV_K - back, :]
        conv = conv + tap * pltpu.roll(ext, back, axis=0)[halo:]
    yc_ref[0] = (gate_b * (conv + cb_ref[...])).astype(_BF16)
    cv_ref[...] = cv[tm - halo:]

    n_stage = nb - 1
    q = _head_rmsnorm(proj(0, 1), qg_ref[...]) * (HEAD_DIM ** -0.5 * LOG2_E)
    _store_dilated(q, q_refs, stage_refs[0:n_stage])
    k = _head_rmsnorm(proj(1, 2), kg_ref[...])
    _store_dilated(k, k_refs, stage_refs[n_stage:2 * n_stage])
    _store_dilated(proj(2, 3), v_refs, stage_refs[2 * n_stage:3 * n_stage])


def _mixer_proj(x, g, shift, scale, w_in, lead, qg, kg, conv_w, conv_b, tm=1024):
    b, s, d = x.shape
    a = conv_w.shape[-1]
    row = pl.BlockSpec((1, tm, d), lambda i, j: (i, j, 0))
    vec = pl.BlockSpec((1, 1, d), lambda i, j: (i, 0, 0))
    small = lambda shape: pl.BlockSpec(shape, lambda i, j: (0, 0))
    n_hp = a // LANES
    heads = [pl.BlockSpec((1, n_hp, tm // dl, dl * LANES), lambda i, j: (i, 0, j, 0))
             for dl in DILATIONS]
    qkv_shapes = [jax.ShapeDtypeStruct((b, n_hp, s // dl, dl * LANES), _BF16)
                  for dl in DILATIONS]
    outs = pl.pallas_call(
        _proj_kernel,
        grid=(b, s // tm),
        in_specs=[row, small((1, d)), vec, vec, _resident(w_in.shape, lead),
                  small((1, a)), small((1, a)), small((CONV_K, a)), small((1, a))],
        out_specs=heads * 3 + [pl.BlockSpec((1, tm, a), lambda i, j: (i, j, 0))],
        out_shape=qkv_shapes * 3 + [jax.ShapeDtypeStruct((b, s, a), _BF16)],
        scratch_shapes=([pltpu.VMEM((SUBLANES, a), _F32)]
                        + [pltpu.VMEM((n_hp * tm, LANES), _F32)] * (3 * (len(DILATIONS) - 1))),
        compiler_params=_params(2),
        name="mixer_in_proj",
    )(x, g, shift, scale, w_in, qg, kg, conv_w, conv_b)
    nb = len(DILATIONS)
    return outs[0:nb], outs[nb:2 * nb], outs[2 * nb:3 * nb], outs[3 * nb]


def _band_bias():
    n = ATTN_BLOCK
    qi = jnp.arange(n)[:, None]
    kj = jnp.arange(2 * n)[None, :]
    dist = qi + n - kj
    band = (dist >= 0) & (dist <= n)
    masks = jnp.stack([band, band & (kj >= n)])
    return jnp.where(masks, 0.0, NEG).astype(_F32)


def _attn_unit(qb, kprev, kcur, vprev, vcur, bias, row_max):
    n = ATTN_BLOCK
    lane = lax.broadcasted_iota(jnp.int32, (n, LANES), 1)
    head0 = lane < HEAD_DIM
    zero = jnp.zeros_like(qb)
    q2 = jnp.concatenate([jnp.where(head0, qb, zero), jnp.where(head0, zero, qb)], axis=0)
    kk = jnp.concatenate([kprev, kcur], axis=0)
    s = lax.dot_general(q2, kk, (((1,), (1,)), ((), ())),
                        preferred_element_type=_F32)
    s = jnp.concatenate([s[:n] + bias, s[n:] + bias], axis=0)
    if row_max:
        m = jnp.max(s, axis=-1, keepdims=True)
        s = s - m
    p = jnp.exp2(s).astype(_BF16)
    vv = jnp.concatenate([vprev, vcur], axis=0)
    vext = jnp.concatenate([vv, jnp.ones_like(vv)], axis=1)
    o = jnp.dot(p, vext, preferred_element_type=_F32)
    stats = [jnp.where(head0, o[:n, LANES:], o[n:, LANES:]),
             jnp.where(head0, o[:n, :LANES], o[n:, :LANES])]
    if row_max:
        stats.insert(0, jnp.where(head0, jnp.broadcast_to(m[:n], (n, LANES)),
                                  jnp.broadcast_to(m[n:], (n, LANES))))
    return stats


def _attn_kernel(*refs, row_max):
    nb = len(DILATIONS)
    bias_ref, y_ref = refs[5 * nb], refs[5 * nb + 1]
    scratch = refs[5 * nb + 2:]
    per_hp = len(scratch) // ATTN_HEAD_PAIRS
    for hp in range(ATTN_HEAD_PAIRS):
        _attn_head_pair(hp, refs[:5 * nb], bias_ref, y_ref,
                        scratch[hp * per_hp:(hp + 1) * per_hp], row_max)


def _attn_head_pair(hp, in_refs, bias_ref, y_ref, stat_refs, row_max):
    nb = len(DILATIONS)
    q_refs, k_refs, kh_refs, v_refs, vh_refs = (
        [ref.at[hp] for ref in in_refs[i * nb:(i + 1) * nb]] for i in range(5))
    n_stat = len(stat_refs) // (nb - 1)
    n = ATTN_BLOCK
    halo_slab = jnp.where(pl.program_id(2) > 0, 0, 1)

    def block_stats(bi, j, r):
        q_ref, k_ref, kh_ref, v_ref, vh_ref = (
            q_refs[bi], k_refs[bi], kh_refs[bi], v_refs[bi], vh_refs[bi])
        cols = slice(r * LANES, (r + 1) * LANES)
        cur = slice(j * n, (j + 1) * n)
        if j == 0:
            kp, vp, bias = kh_ref[:, cols], vh_ref[:, cols], bias_ref[halo_slab]
        else:
            prev = slice((j - 1) * n, j * n)
            kp, vp, bias = k_ref[prev, cols], v_ref[prev, cols], bias_ref[0]
        return _attn_unit(q_ref[cur, cols], kp, k_ref[cur, cols], vp, v_ref[cur, cols],
                          bias, row_max)

    for bi in range(nb - 1, 0, -1):
        d = DILATIONS[bi]
        for r in range(d):
            for j in range(ATTN_TILE // (d * n)):
                rows = pl.ds(j * n * d + r, n, stride=d)
                for ref, val in zip(stat_refs[n_stat * (bi - 1):n_stat * bi],
                                    block_stats(bi, j, r)):
                    ref[rows, :] = val

    for j in range(ATTN_TILE // n):
        rows = slice(j * n, (j + 1) * n)
        stats = [block_stats(0, j, 0)] + [
            [ref[rows, :] for ref in stat_refs[n_stat * (bi - 1):n_stat * bi]]
            for bi in range(1, nb)]
        if row_max:
            m = functools.reduce(jnp.maximum, [st[0] for st in stats])
            ws = [jnp.exp2(st[0] - m) for st in stats]
            den = sum(w * st[1] for w, st in zip(ws, stats))
            num = sum(w * st[2] for w, st in zip(ws, stats))
        else:
            den = sum(st[0] for st in stats)
            num = sum(st[1] for st in stats)
        y_ref[rows, hp * LANES:(hp + 1) * LANES] = (num / den).astype(y_ref.dtype)


def _dilated_attention(qs, ks, vs, row_max):
    b, hp, s, _ = qs[0].shape
    n = ATTN_BLOCK

    def tile_spec(d):
        return pl.BlockSpec((None, ATTN_HEAD_PAIRS, ATTN_TILE // d, d * LANES),
                            lambda i, h, t: (i, h, t, 0))

    def halo_spec(d):
        per_tile = ATTN_TILE // (d * n)
        return pl.BlockSpec((None, ATTN_HEAD_PAIRS, n, d * LANES),
                            lambda i, h, t: (i, h, jnp.maximum(t * per_tile - 1, 0), 0))

    tiles = [tile_spec(d) for d in DILATIONS]
    halos = [halo_spec(d) for d in DILATIONS]
    args = list(qs) + list(ks) * 2 + list(vs) * 2 + [_band_bias()]
    return pl.pallas_call(
        functools.partial(_attn_kernel, row_max=row_max),
        grid=(b, hp // ATTN_HEAD_PAIRS, s // ATTN_TILE),
        in_specs=tiles + tiles + halos + tiles + halos + [_resident((2, n, 2 * n))],
        out_specs=pl.BlockSpec((None, ATTN_TILE, ATTN_HEAD_PAIRS * LANES),
                               lambda i, h, t: (i, t, h)),
        out_shape=jax.ShapeDtypeStruct((b, s, hp * LANES), _BF16),
        scratch_shapes=([pltpu.VMEM((ATTN_TILE, LANES), _F32)]
                        * (ATTN_HEAD_PAIRS * (3 if row_max else 2) * (len(DILATIONS) - 1))),
        compiler_params=_params(3),
        name="dilated_attention",
    )(*args)


def kernel(x, c, w_ada, b_ada, norm_g, w_in, q_norm_g, k_norm_g, conv_w, conv_b,
           w_out, ffn_w1, ffn_w2):
    b, s, d = x.shape
    depth = w_ada.shape[0]
    a = conv_w.shape[-1]
    n_heads = a // HEAD_DIM
    assert d - a == a and a % (ATTN_HEAD_PAIRS * LANES) == 0
    assert s % ATTN_TILE == 0 and DILATIONS[0] == 1
    assert w_ada.shape[-1] == N_SUB * N_MOD * d

    rows = -(-b // SUBLANES) * SUBLANES
    c_pad = jnp.pad(c, ((0, rows - b), (0, 0)))
    mod = _modulation(c_pad, w_ada, b_ada)[:, :b].reshape(depth, b, N_SUB, N_MOD, 1, d)

    w1 = ffn_w1.astype(_BF16)
    w2 = ffn_w2.astype(_BF16)
    w_in_b = w_in.astype(_BF16)
    w_out_b = w_out.astype(_BF16)

    for layer in range(depth):
        shift = lambda i: mod[layer, :, i, 0]
        scale = lambda i: mod[layer, :, i, 1]
        gate = lambda i: mod[layer, :, i, 2]
        g = lambda i: norm_g[layer, i][None, :]

        x = _ffn(x, g(0), shift(0), scale(0), gate(0), w1, w2, (layer, 0))
        qs, ks, vs, yc = _mixer_proj(
            x, g(1), shift(1), scale(1), w_in_b, (layer,),
            jnp.tile(q_norm_g[layer], n_heads)[None, :],
            jnp.tile(k_norm_g[layer], n_heads)[None, :],
            conv_w[layer], conv_b[layer][None, :])
        score_bound = (HEAD_DIM ** 0.5 * LOG2_E * jnp.max(jnp.abs(q_norm_g[layer]))
                       * jnp.max(jnp.abs(k_norm_g[layer])))
        ya = lax.cond(score_bound <= MAX_UNSHIFTED_SCORE,
                      functools.partial(_dilated_attention, row_max=False),
                      functools.partial(_dilated_attention, row_max=True),
                      qs, ks, vs)
        x = _ffn(x, g(2), shift(2), scale(2), gate(2), w1, w2, (layer, 1),
                 mixer=(ya, yc, gate(1), w_out_b, (layer,)))
    return x
```

```python
import functools

import jax
import jax.numpy as jnp
from jax import lax
from jax.experimental import pallas as pl
from jax.experimental.pallas import tpu as pltpu

EPS = 1e-6
NEG = -1e30
LOG2_E = 1.4426950408889634
MAX_UNSHIFTED_SCORE = 60.0
HEAD_DIM = 64
N_SUB = 3
N_MOD = 3
CONV_K = 3
DILATIONS = (1, 4, 16)
ATTN_BLOCK = 128
LANES = 128
SUBLANES = 8
ATTN_TILE = ATTN_BLOCK * DILATIONS[-1]
ATTN_HEAD_PAIRS = 2
ROW_TILE = 1024
FFN_ROW_GROUPS = 4
FFN_MIX_ROW_GROUPS = 2
VMEM_LIMIT_BYTES = 56 * 1024 * 1024

_BF16 = jnp.bfloat16
_F32 = jnp.float32


def _params(n_axes):
    return pltpu.CompilerParams(
        dimension_semantics=("arbitrary",) * n_axes,
        vmem_limit_bytes=VMEM_LIMIT_BYTES)


def _resident(shape, lead=()):
    nd = len(shape) - len(lead)
    return pl.BlockSpec((None,) * len(lead) + tuple(shape[len(lead):]),
                        lambda *_: tuple(lead) + (0,) * nd,
                        pipeline_mode=pl.Buffered(1))


def _silu(z):
    return z * (1.0 / (1.0 + jnp.exp(-z)))


def _ada_norm(x, g, shift, scale):
    y = x * lax.rsqrt(jnp.mean(x * x, axis=-1, keepdims=True) + EPS)
    return (y * g) * (1.0 + scale) + shift


def _mod_kernel(c_ref, w_ref, b_ref, o_ref):
    a = _silu(c_ref[...]).astype(_BF16)
    o_ref[0] = jnp.dot(a, w_ref[0].astype(_BF16),
                       preferred_element_type=_F32) + b_ref[0]


def _modulation(c_pad, w_ada, b_ada):
    depth, d, n = w_ada.shape
    rows = c_pad.shape[0]
    tn = n // N_SUB
    return pl.pallas_call(
        _mod_kernel,
        grid=(depth, n // tn),
        in_specs=[
            pl.BlockSpec((rows, d), lambda l, j: (0, 0)),
            pl.BlockSpec((1, d, tn), lambda l, j: (l, 0, j)),
            pl.BlockSpec((1, 1, tn), lambda l, j: (l, 0, j)),
        ],
        out_specs=pl.BlockSpec((1, rows, tn), lambda l, j: (l, 0, j)),
        out_shape=jax.ShapeDtypeStruct((depth, rows, n), _F32),
        compiler_params=_params(2),
        name="ada_modulation",
    )(c_pad, w_ada, b_ada.reshape(depth, 1, n))


def _ffn_kernel(x_ref, g_ref, shift_ref, scale_ref, gate_ref, w1_ref, w2_ref, *refs):
    o_ref = refs[-1]
    tm = x_ref.shape[1]
    f = w2_ref.shape[0]
    group = tm // (FFN_MIX_ROW_GROUPS if len(refs) > 1 else FFN_ROW_GROUPS)
    for lo in range(0, tm, group):
        rows = slice(lo, lo + group)
        x = x_ref[0, rows]
        if len(refs) > 1:
            ya_ref, yc_ref, gate_mix_ref, wo_ref = refs[:-1]
            a = ya_ref.shape[2]
            mix = (jnp.dot(ya_ref[0, rows], wo_ref[0:a, :], preferred_element_type=_F32)
                   + jnp.dot(yc_ref[0, rows], wo_ref[a:, :], preferred_element_type=_F32))
            x = x + gate_mix_ref[0] * mix
        h = _ada_norm(x, g_ref[...], shift_ref[0], scale_ref[0]).astype(_BF16)
        gu = jnp.dot(h, w1_ref[...], preferred_element_type=_F32)
        act = (_silu(gu[:, :f]) * gu[:, f:]).astype(_BF16)
        y = jnp.dot(act, w2_ref[...], preferred_element_type=_F32)
        o_ref[0, rows] = x + (0.5 * gate_ref[0]) * y


def _ffn(x, g, shift, scale, gate, w1, w2, lead, mixer=None):
    b, s, d = x.shape
    tm = ROW_TILE
    row = pl.BlockSpec((1, tm, d), lambda i, j: (i, j, 0))
    vec = pl.BlockSpec((1, 1, d), lambda i, j: (i, 0, 0))
    in_specs = [row, pl.BlockSpec((1, d), lambda i, j: (0, 0)), vec, vec, vec,
                _resident(w1.shape, lead), _resident(w2.shape, lead)]
    args = [x, g, shift, scale, gate, w1, w2]
    if mixer is not None:
        ya, yc, gate_mix, w_out, w_out_lead = mixer
        half = pl.BlockSpec((1, tm, ya.shape[2]), lambda i, j: (i, j, 0))
        in_specs += [half, half, vec, _resident(w_out.shape, w_out_lead)]
        args += [ya, yc, gate_mix, w_out]
    return pl.pallas_call(
        _ffn_kernel,
        grid=(b, s // tm),
        in_specs=in_specs,
        out_specs=row,
        out_shape=jax.ShapeDtypeStruct(x.shape, x.dtype),
        compiler_params=_params(2),
        name="swiglu_half_step",
    )(*args)


def _head_rmsnorm(z, gain):
    rows, a = z.shape
    first = lax.broadcasted_iota(jnp.int32, (rows, LANES), 1) < HEAD_DIM
    out = []
    for c in range(0, a, LANES):
        zc = z[:, c:c + LANES]
        zz = zc * zc
        s0 = jnp.sum(jnp.where(first, zz, 0.0), axis=-1, keepdims=True)
        s1 = jnp.sum(jnp.where(first, 0.0, zz), axis=-1, keepdims=True)
        ms = jnp.where(first, s0, s1) * (1.0 / HEAD_DIM)
        out.append(zc * lax.rsqrt(ms + EPS))
    return jnp.concatenate(out, axis=1) * gain


def _store_dilated(z, out_refs, stage_refs):
    tm, a = z.shape
    n_hp = a // LANES
    for hp in range(n_hp):
        zc = z[:, hp * LANES:(hp + 1) * LANES]
        out_refs[0][0, hp] = zc.astype(_BF16)
        stage_refs[0][hp * tm:(hp + 1) * tm, :] = zc
    for i in range(1, len(DILATIONS)):
        d_prev, d = DILATIONS[i - 1], DILATIONS[i]
        step = d // d_prev
        rows_prev, rows = tm // d_prev, tm // d
        for hp in range(n_hp):
            for r_prev in range(d_prev):
                base = (hp * d_prev + r_prev) * rows_prev
                for sub in range(step):
                    r = r_prev + d_prev * sub
                    piece = stage_refs[i - 1][pl.ds(base + sub, rows, stride=step), :]
                    out_refs[i][0, hp, :, r * LANES:(r + 1) * LANES] = piece.astype(_BF16)
                    if i + 1 < len(DILATIONS):
                        dst = (hp * d + r) * rows
                        stage_refs[i][dst:dst + rows, :] = piece


def _proj_kernel(x_ref, g_ref, shift_ref, scale_ref, w_in_ref, qg_ref, kg_ref,
                 cw_ref, cb_ref, *refs):
    nb = len(DILATIONS)
    q_refs, k_refs, v_refs = refs[0:nb], refs[nb:2 * nb], refs[2 * nb:3 * nb]
    yc_ref, cv_ref = refs[3 * nb], refs[3 * nb + 1]
    stage_refs = refs[3 * nb + 2:]
    tm = x_ref.shape[1]
    a = yc_ref.shape[2]
    halo = SUBLANES

    @pl.when(pl.program_id(1) == 0)
    def _():
        cv_ref[...] = jnp.zeros((halo, a), _F32)

    h = _ada_norm(x_ref[0], g_ref[...], shift_ref[0], scale_ref[0]).astype(_BF16)

    def proj(lo, hi):
        return jnp.dot(h, w_in_ref[:, lo * a:hi * a], preferred_element_type=_F32)

    conv_in = proj(3, 6)
    gate_b = conv_in[:, 0:a]
    cv = conv_in[:, a:2 * a] * conv_in[:, 2 * a:3 * a]
    ext = jnp.concatenate([cv_ref[...], cv], axis=0)
    conv = cw_ref[CONV_K - 1:CONV_K, :] * cv
    for back in range(1, CONV_K):
        tap = cw_ref[CONV_K - 1 - back:CONV_K - back, :]
        conv = conv + tap * pltpu.roll(ext, back, axis=0)[halo:]
    yc_ref[0] = (gate_b * (conv + cb_ref[...])).astype(_BF16)
    cv_ref[...] = cv[tm - halo:]

    n_stage = nb - 1
    q = _head_rmsnorm(proj(0, 1), qg_ref[...]) * (HEAD_DIM ** -0.5 * LOG2_E)
    _store_dilated(q, q_refs, stage_refs[0:n_stage])
    k = _head_rmsnorm(proj(1, 2), kg_ref[...])
    _store_dilated(k, k_refs, stage_refs[n_stage:2 * n_stage])
    _store_dilated(proj(2, 3), v_refs, stage_refs[2 * n_stage:3 * n_stage])


def _mixer_proj(x, g, shift, scale, w_in, lead, qg, kg, conv_w, conv_b):
    b, s, d = x.shape
    tm = ROW_TILE
    a = conv_w.shape[-1]
    row = pl.BlockSpec((1, tm, d), lambda i, j: (i, j, 0))
    vec = pl.BlockSpec((1, 1, d), lambda i, j: (i, 0, 0))
    small = lambda shape: pl.BlockSpec(shape, lambda i, j: (0, 0))
    n_hp = a // LANES
    heads = [pl.BlockSpec((1, n_hp, tm // dl, dl * LANES), lambda i, j: (i, 0, j, 0))
             for dl in DILATIONS]
    qkv_shapes = [jax.ShapeDtypeStruct((b, n_hp, s // dl, dl * LANES), _BF16)
                  for dl in DILATIONS]
    outs = pl.pallas_call(
        _proj_kernel,
        grid=(b, s // tm),
        in_specs=[row, small((1, d)), vec, vec, _resident(w_in.shape, lead),
                  small((1, a)), small((1, a)), small((CONV_K, a)), small((1, a))],
        out_specs=heads * 3 + [pl.BlockSpec((1, tm, a), lambda i, j: (i, j, 0))],
        out_shape=qkv_shapes * 3 + [jax.ShapeDtypeStruct((b, s, a), _BF16)],
        scratch_shapes=([pltpu.VMEM((SUBLANES, a), _F32)]
                        + [pltpu.VMEM((n_hp * tm, LANES), _F32)] * (3 * (len(DILATIONS) - 1))),
        compiler_params=_params(2),
        name="mixer_in_proj",
    )(x, g, shift, scale, w_in, qg, kg, conv_w, conv_b)
    nb = len(DILATIONS)
    return outs[0:nb], outs[nb:2 * nb], outs[2 * nb:3 * nb], outs[3 * nb]


def _band_bias():
    n = ATTN_BLOCK
    qi = jnp.arange(n)[:, None]
    kj = jnp.arange(2 * n)[None, :]
    dist = qi + n - kj
    band = (dist >= 0) & (dist <= n)
    masks = jnp.stack([band, band & (kj >= n)])
    return jnp.where(masks, 0.0, NEG).astype(_F32)


def _attn_unit(qb, kprev, kcur, vprev, vcur, bias, row_max):
    n = ATTN_BLOCK
    lane = lax.broadcasted_iota(jnp.int32, (n, LANES), 1)
    head0 = lane < HEAD_DIM
    zero = jnp.zeros_like(qb)
    q2 = jnp.concatenate([jnp.where(head0, qb, zero), jnp.where(head0, zero, qb)], axis=0)
    kk = jnp.concatenate([kprev, kcur], axis=0)
    s = lax.dot_general(q2, kk, (((1,), (1,)), ((), ())),
                        preferred_element_type=_F32)
    s = jnp.concatenate([s[:n] + bias, s[n:] + bias], axis=0)
    if row_max:
        m = jnp.max(s, axis=-1, keepdims=True)
        s = s - m
    p = jnp.exp2(s).astype(_BF16)
    vv = jnp.concatenate([vprev, vcur], axis=0)
    vext = jnp.concatenate([vv, jnp.ones_like(vv)], axis=1)
    o = jnp.dot(p, vext, preferred_element_type=_F32)
    stats = [jnp.where(head0, o[:n, LANES:], o[n:, LANES:]),
             jnp.where(head0, o[:n, :LANES], o[n:, :LANES])]
    if row_max:
        stats.insert(0, jnp.where(head0, jnp.broadcast_to(m[:n], (n, LANES)),
                                  jnp.broadcast_to(m[n:], (n, LANES))))
    return stats


def _attn_kernel(*refs, row_max):
    nb = len(DILATIONS)
    bias_ref, y_ref = refs[5 * nb], refs[5 * nb + 1]
    scratch = refs[5 * nb + 2:]
    per_hp = len(scratch) // ATTN_HEAD_PAIRS
    for hp in range(ATTN_HEAD_PAIRS):
        _attn_head_pair(hp, refs[:5 * nb], bias_ref, y_ref,
                        scratch[hp * per_hp:(hp + 1) * per_hp], row_max)


def _attn_head_pair(hp, in_refs, bias_ref, y_ref, stat_refs, row_max):
    nb = len(DILATIONS)
    q_refs, k_refs, kh_refs, v_refs, vh_refs = (
        [ref.at[hp] for ref in in_refs[i * nb:(i + 1) * nb]] for i in range(5))
    n_stat = len(stat_refs) // (nb - 1)
    n = ATTN_BLOCK
    halo_slab = jnp.where(pl.program_id(2) > 0, 0, 1)

    def block_stats(bi, j, r):
        q_ref, k_ref, kh_ref, v_ref, vh_ref = (
            q_refs[bi], k_refs[bi], kh_refs[bi], v_refs[bi], vh_refs[bi])
        cols = slice(r * LANES, (r + 1) * LANES)
        cur = slice(j * n, (j + 1) * n)
        if j == 0:
            kp, vp, bias = kh_ref[:, cols], vh_ref[:, cols], bias_ref[halo_slab]
        else:
            prev = slice((j - 1) * n, j * n)
            kp, vp, bias = k_ref[prev, cols], v_ref[prev, cols], bias_ref[0]
        return _attn_unit(q_ref[cur, cols], kp, k_ref[cur, cols], vp, v_ref[cur, cols],
                          bias, row_max)

    for bi in range(nb - 1, 0, -1):
        d = DILATIONS[bi]
        for r in range(d):
            for j in range(ATTN_TILE // (d * n)):
                rows = pl.ds(j * n * d + r, n, stride=d)
                for ref, val in zip(stat_refs[n_stat * (bi - 1):n_stat * bi],
                                    block_stats(bi, j, r)):
                    ref[rows, :] = val

    for j in range(ATTN_TILE // n):
        rows = slice(j * n, (j + 1) * n)
        stats = [block_stats(0, j, 0)] + [
            [ref[rows, :] for ref in stat_refs[n_stat * (bi - 1):n_stat * bi]]
            for bi in range(1, nb)]
        if row_max:
            m = functools.reduce(jnp.maximum, [st[0] for st in stats])
            ws = [jnp.exp2(st[0] - m) for st in stats]
            den = sum(w * st[1] for w, st in zip(ws, stats))
            num = sum(w * st[2] for w, st in zip(ws, stats))
        else:
            den = sum(st[0] for st in stats)
            num = sum(st[1] for st in stats)
        y_ref[rows, hp * LANES:(hp + 1) * LANES] = (num / den).astype(y_ref.dtype)


def _dilated_attention(qs, ks, vs, row_max):
    b, hp, s, _ = qs[0].shape
    n = ATTN_BLOCK

    def tile_spec(d):
        return pl.BlockSpec((None, ATTN_HEAD_PAIRS, ATTN_TILE // d, d * LANES),
                            lambda i, h, t: (i, h, t, 0))

    def halo_spec(d):
        per_tile = ATTN_TILE // (d * n)
        return pl.BlockSpec((None, ATTN_HEAD_PAIRS, n, d * LANES),
                            lambda i, h, t: (i, h, jnp.maximum(t * per_tile - 1, 0), 0))

    tiles = [tile_spec(d) for d in DILATIONS]
    halos = [halo_spec(d) for d in DILATIONS]
    args = list(qs) + list(ks) * 2 + list(vs) * 2 + [_band_bias()]
    return pl.pallas_call(
        functools.partial(_attn_kernel, row_max=row_max),
        grid=(b, hp // ATTN_HEAD_PAIRS, s // ATTN_TILE),
        in_specs=tiles + tiles + halos + tiles + halos + [_resident((2, n, 2 * n))],
        out_specs=pl.BlockSpec((None, ATTN_TILE, ATTN_HEAD_PAIRS * LANES),
                               lambda i, h, t: (i, t, h)),
        out_shape=jax.ShapeDtypeStruct((b, s, hp * LANES), _BF16),
        scratch_shapes=([pltpu.VMEM((ATTN_TILE, LANES), _F32)]
                        * (ATTN_HEAD_PAIRS * (3 if row_max else 2) * (len(DILATIONS) - 1))),
        compiler_params=_params(3),
        name="dilated_attention",
    )(*args)


def kernel(x, c, w_ada, b_ada, norm_g, w_in, q_norm_g, k_norm_g, conv_w, conv_b,
           w_out, ffn_w1, ffn_w2):
    b, s, d = x.shape
    depth = w_ada.shape[0]
    a = conv_w.shape[-1]
    n_heads = a // HEAD_DIM
    assert d - a == a and a % (ATTN_HEAD_PAIRS * LANES) == 0
    assert s % ATTN_TILE == 0 and s % ROW_TILE == 0 and DILATIONS[0] == 1
    assert ROW_TILE % (DILATIONS[-1] * 2 * SUBLANES) == 0
    assert q_norm_g.shape[-1] == HEAD_DIM and conv_w.shape[1] == CONV_K
    assert w_ada.shape[-1] == N_SUB * N_MOD * d

    rows = -(-b // SUBLANES) * SUBLANES
    c_pad = jnp.pad(c, ((0, rows - b), (0, 0)))
    mod = _modulation(c_pad, w_ada, b_ada)[:, :b].reshape(depth, b, N_SUB, N_MOD, 1, d)

    w1 = ffn_w1.astype(_BF16)
    w2 = ffn_w2.astype(_BF16)
    w_in_b = w_in.astype(_BF16)
    w_out_b = w_out.astype(_BF16)

    for layer in range(depth):
        shift = lambda i: mod[layer, :, i, 0]
        scale = lambda i: mod[layer, :, i, 1]
        gate = lambda i: mod[layer, :, i, 2]
        g = lambda i: norm_g[layer, i][None, :]

        x = _ffn(x, g(0), shift(0), scale(0), gate(0), w1, w2, (layer, 0))
        qs, ks, vs, yc = _mixer_proj(
            x, g(1), shift(1), scale(1), w_in_b, (layer,),
            jnp.tile(q_norm_g[layer], n_heads)[None, :],
            jnp.tile(k_norm_g[layer], n_heads)[None, :],
            conv_w[layer], conv_b[layer][None, :])
        score_bound = (HEAD_DIM ** 0.5 * LOG2_E * jnp.max(jnp.abs(q_norm_g[layer]))
                       * jnp.max(jnp.abs(k_norm_g[layer])))
        ya = lax.cond(score_bound <= MAX_UNSHIFTED_SCORE,
                      functools.partial(_dilated_attention, row_max=False),
                      functools.partial(_dilated_attention, row_max=True),
                      qs, ks, vs)
        x = _ffn(x, g(2), shift(2), scale(2), gate(2), w1, w2, (layer, 1),
                 mixer=(ya, yc, gate(1), w_out_b, (layer,)))
    return x
```

```python
import functools

import jax
import jax.numpy as jnp
from jax import lax
from jax.experimental import pallas as pl
from jax.experimental.pallas import tpu as pltpu

EPS = 1e-6
NEG = -1e30
LOG2_E = 1.4426950408889634
MAX_UNSHIFTED_SCORE = 60.0
HEAD_DIM = 64
N_SUB = 3
N_MOD = 3
CONV_K = 3
DILATIONS = (1, 4, 16)
ATTN_BLOCK = 128
LANES = 128
SUBLANES = 8
ATTN_TILE = ATTN_BLOCK * DILATIONS[-1]
ATTN_HEAD_PAIRS = 2
ROW_TILE = 1024
FFN_ROW_GROUPS = 8
FFN_MIX_ROW_GROUPS = 2
VMEM_LIMIT_BYTES = 56 * 1024 * 1024

_BF16 = jnp.bfloat16
_F32 = jnp.float32


def _params(n_axes):
    return pltpu.CompilerParams(
        dimension_semantics=("arbitrary",) * n_axes,
        vmem_limit_bytes=VMEM_LIMIT_BYTES)


def _resident(shape, lead=()):
    nd = len(shape) - len(lead)
    return pl.BlockSpec((None,) * len(lead) + tuple(shape[len(lead):]),
                        lambda *_: tuple(lead) + (0,) * nd,
                        pipeline_mode=pl.Buffered(1))


def _silu(z):
    return z * (1.0 / (1.0 + jnp.exp(-z)))


def _ada_norm(x, g, shift, scale):
    y = x * lax.rsqrt(jnp.mean(x * x, axis=-1, keepdims=True) + EPS)
    return (y * g) * (1.0 + scale) + shift


def _mod_kernel(c_ref, w_ref, b_ref, o_ref):
    a = _silu(c_ref[...]).astype(_BF16)
    o_ref[0] = jnp.dot(a, w_ref[0].astype(_BF16),
                       preferred_element_type=_F32) + b_ref[0]


def _modulation(c_pad, w_ada, b_ada):
    depth, d, n = w_ada.shape
    rows = c_pad.shape[0]
    tn = n // N_SUB
    return pl.pallas_call(
        _mod_kernel,
        grid=(depth, n // tn),
        in_specs=[
            pl.BlockSpec((rows, d), lambda l, j: (0, 0)),
            pl.BlockSpec((1, d, tn), lambda l, j: (l, 0, j)),
            pl.BlockSpec((1, 1, tn), lambda l, j: (l, 0, j)),
        ],
        out_specs=pl.BlockSpec((1, rows, tn), lambda l, j: (l, 0, j)),
        out_shape=jax.ShapeDtypeStruct((depth, rows, n), _F32),
        compiler_params=_params(2),
        name="ada_modulation",
    )(c_pad, w_ada, b_ada.reshape(depth, 1, n))


def _ffn_kernel(x_ref, g_ref, shift_ref, scale_ref, gate_ref, w1_ref, w2_ref, *refs):
    o_ref = refs[-1]
    tm = x_ref.shape[1]
    f = w2_ref.shape[0]
    group = tm // (FFN_MIX_ROW_GROUPS if len(refs) > 1 else FFN_ROW_GROUPS)
    for lo in range(0, tm, group):
        rows = slice(lo, lo + group)
        x = x_ref[0, rows]
        if len(refs) > 1:
            ya_ref, yc_ref, gate_mix_ref, wo_ref = refs[:-1]
            a = ya_ref.shape[2]
            mix = (jnp.dot(ya_ref[0, rows], wo_ref[0:a, :], preferred_element_type=_F32)
                   + jnp.dot(yc_ref[0, rows], wo_ref[a:, :], preferred_element_type=_F32))
            x = x + gate_mix_ref[0] * mix
        h = _ada_norm(x, g_ref[...], shift_ref[0], scale_ref[0]).astype(_BF16)
        gu = jnp.dot(h, w1_ref[...], preferred_element_type=_F32)
        act = (_silu(gu[:, :f]) * gu[:, f:]).astype(_BF16)
        y = jnp.dot(act, w2_ref[...], preferred_element_type=_F32)
        o_ref[0, rows] = x + (0.5 * gate_ref[0]) * y


def _ffn(x, g, shift, scale, gate, w1, w2, lead, mixer=None):
    b, s, d = x.shape
    tm = ROW_TILE
    row = pl.BlockSpec((1, tm, d), lambda i, j: (i, j, 0))
    vec = pl.BlockSpec((1, 1, d), lambda i, j: (i, 0, 0))
    in_specs = [row, pl.BlockSpec((1, d), lambda i, j: (0, 0)), vec, vec, vec,
                _resident(w1.shape, lead), _resident(w2.shape, lead)]
    args = [x, g, shift, scale, gate, w1, w2]
    if mixer is not None:
        ya, yc, gate_mix, w_out, w_out_lead = mixer
        half = pl.BlockSpec((1, tm, ya.shape[2]), lambda i, j: (i, j, 0))
        in_specs += [half, half, vec, _resident(w_out.shape, w_out_lead)]
        args += [ya, yc, gate_mix, w_out]
    return pl.pallas_call(
        _ffn_kernel,
        grid=(b, s // tm),
        in_specs=in_specs,
        out_specs=row,
        out_shape=jax.ShapeDtypeStruct(x.shape, x.dtype),
        compiler_params=_params(2),
        name="swiglu_half_step",
    )(*args)


def _head_rmsnorm(z, gain):
    rows, a = z.shape
    first = lax.broadcasted_iota(jnp.int32, (rows, LANES), 1) < HEAD_DIM
    out = []
    for c in range(0, a, LANES):
        zc = z[:, c:c + LANES]
        zz = zc * zc
        s0 = jnp.sum(jnp.where(first, zz, 0.0), axis=-1, keepdims=True)
        s1 = jnp.sum(jnp.where(first, 0.0, zz), axis=-1, keepdims=True)
        ms = jnp.where(first, s0, s1) * (1.0 / HEAD_DIM)
        out.append(zc * lax.rsqrt(ms + EPS))
    return jnp.concatenate(out, axis=1) * gain


def _store_dilated(z, out_refs, stage_refs):
    tm, a = z.shape
    n_hp = a // LANES
    for hp in range(n_hp):
        zc = z[:, hp * LANES:(hp + 1) * LANES]
        out_refs[0][0, hp] = zc.astype(_BF16)
        stage_refs[0][hp * tm:(hp + 1) * tm, :] = zc
    for i in range(1, len(DILATIONS)):
        d_prev, d = DILATIONS[i - 1], DILATIONS[i]
        step = d // d_prev
        rows_prev, rows = tm // d_prev, tm // d
        for hp in range(n_hp):
            for r_prev in range(d_prev):
                base = (hp * d_prev + r_prev) * rows_prev
                for sub in range(step):
                    r = r_prev + d_prev * sub
                    piece = stage_refs[i - 1][pl.ds(base + sub, rows, stride=step), :]
                    out_refs[i][0, hp, :, r * LANES:(r + 1) * LANES] = piece.astype(_BF16)
                    if i + 1 < len(DILATIONS):
                        dst = (hp * d + r) * rows
                        stage_refs[i][dst:dst + rows, :] = piece


def _proj_kernel(x_ref, g_ref, shift_ref, scale_ref, w_in_ref, qg_ref, kg_ref,
                 cw_ref, cb_ref, *refs):
    nb = len(DILATIONS)
    q_refs, k_refs, v_refs = refs[0:nb], refs[nb:2 * nb], refs[2 * nb:3 * nb]
    yc_ref, cv_ref = refs[3 * nb], refs[3 * nb + 1]
    stage_refs = refs[3 * nb + 2:]
    tm = x_ref.shape[1]
    a = yc_ref.shape[2]
    halo = SUBLANES

    @pl.when(pl.program_id(1) == 0)
    def _():
        cv_ref[...] = jnp.zeros((halo, a), _F32)

    h = _ada_norm(x_ref[0], g_ref[...], shift_ref[0], scale_ref[0]).astype(_BF16)

    def proj(lo, hi):
        return jnp.dot(h, w_in_ref[:, lo * a:hi * a], preferred_element_type=_F32)

    conv_in = proj(3, 6)
    gate_b = conv_in[:, 0:a]
    cv = conv_in[:, a:2 * a] * conv_in[:, 2 * a:3 * a]
    ext = jnp.concatenate([cv_ref[...], cv], axis=0)
    conv = cw_ref[CONV_K - 1:CONV_K, :] * cv
    for back in range(1, CONV_K):
        tap = cw_ref[CONV_K - 1 - back:CONV_K - back, :]
        conv = conv + tap * pltpu.roll(ext, back, axis=0)[halo:]
    yc_ref[0] = (gate_b * (conv + cb_ref[...])).astype(_BF16)
    cv_ref[...] = cv[tm - halo:]

    n_stage = nb - 1
    q = _head_rmsnorm(proj(0, 1), qg_ref[...]) * (HEAD_DIM ** -0.5 * LOG2_E)
    _store_dilated(q, q_refs, stage_refs[0:n_stage])
    k = _head_rmsnorm(proj(1, 2), kg_ref[...])
    _store_dilated(k, k_refs, stage_refs[n_stage:2 * n_stage])
    _store_dilated(proj(2, 3), v_refs, stage_refs[2 * n_stage:3 * n_stage])


def _mixer_proj(x, g, shift, scale, w_in, lead, qg, kg, conv_w, conv_b):
    b, s, d = x.shape
    tm = ROW_TILE
    a = conv_w.shape[-1]
    row = pl.BlockSpec((1, tm, d), lambda i, j: (i, j, 0))
    vec = pl.BlockSpec((1, 1, d), lambda i, j: (i, 0, 0))
    small = lambda shape: pl.BlockSpec(shape, lambda i, j: (0, 0))
    n_hp = a // LANES
    heads = [pl.BlockSpec((1, n_hp, tm // dl, dl * LANES), lambda i, j: (i, 0, j, 0))
             for dl in DILATIONS]
    qkv_shapes = [jax.ShapeDtypeStruct((b, n_hp, s // dl, dl * LANES), _BF16)
                  for dl in DILATIONS]
    outs = pl.pallas_call(
        _proj_kernel,
        grid=(b, s // tm),
        in_specs=[row, small((1, d)), vec, vec, _resident(w_in.shape, lead),
                  small((1, a)), small((1, a)), small((CONV_K, a)), small((1, a))],
        out_specs=heads * 3 + [pl.BlockSpec((1, tm, a), lambda i, j: (i, j, 0))],
        out_shape=qkv_shapes * 3 + [jax.ShapeDtypeStruct((b, s, a), _BF16)],
        scratch_shapes=([pltpu.VMEM((SUBLANES, a), _F32)]
                        + [pltpu.VMEM((n_hp * tm, LANES), _F32)] * (3 * (len(DILATIONS) - 1))),
        compiler_params=_params(2),
        name="mixer_in_proj",
    )(x, g, shift, scale, w_in, qg, kg, conv_w, conv_b)
    nb = len(DILATIONS)
    return outs[0:nb], outs[nb:2 * nb], outs[2 * nb:3 * nb], outs[3 * nb]


def _band_bias():
    n = ATTN_BLOCK
    qi = jnp.arange(n)[:, None]
    kj = jnp.arange(2 * n)[None, :]
    dist = qi + n - kj
    band = (dist >= 0) & (dist <= n)
    masks = jnp.stack([band, band & (kj >= n)])
    return jnp.where(masks, 0.0, NEG).astype(_F32)


def _attn_unit(qb, kprev, kcur, vprev, vcur, bias, row_max):
    n = ATTN_BLOCK
    lane = lax.broadcasted_iota(jnp.int32, (n, LANES), 1)
    head0 = lane < HEAD_DIM
    zero = jnp.zeros_like(qb)
    q2 = jnp.concatenate([jnp.where(head0, qb, zero), jnp.where(head0, zero, qb)], axis=0)
    kk = jnp.concatenate([kprev, kcur], axis=0)
    s = lax.dot_general(q2, kk, (((1,), (1,)), ((), ())),
                        preferred_element_type=_F32)
    s = jnp.concatenate([s[:n] + bias, s[n:] + bias], axis=0)
    if row_max:
        m = jnp.max(s, axis=-1, keepdims=True)
        s = s - m
    p = jnp.exp2(s).astype(_BF16)
    vv = jnp.concatenate([vprev, vcur], axis=0)
    vext = jnp.concatenate([vv, jnp.ones_like(vv)], axis=1)
    o = jnp.dot(p, vext, preferred_element_type=_F32)
    stats = [jnp.where(head0, o[:n, LANES:], o[n:, LANES:]),
             jnp.where(head0, o[:n, :LANES], o[n:, :LANES])]
    if row_max:
        stats.insert(0, jnp.where(head0, jnp.broadcast_to(m[:n], (n, LANES)),
                                  jnp.broadcast_to(m[n:], (n, LANES))))
    return stats


def _attn_kernel(*refs, row_max):
    nb = len(DILATIONS)
    bias_ref, y_ref = refs[5 * nb], refs[5 * nb + 1]
    scratch = refs[5 * nb + 2:]
    per_hp = len(scratch) // ATTN_HEAD_PAIRS
    for hp in range(ATTN_HEAD_PAIRS):
        _attn_head_pair(hp, refs[:5 * nb], bias_ref, y_ref,
                        scratch[hp * per_hp:(hp + 1) * per_hp], row_max)


def _attn_head_pair(hp, in_refs, bias_ref, y_ref, stat_refs, row_max):
    nb = len(DILATIONS)
    q_refs, k_refs, kh_refs, v_refs, vh_refs = (
        [ref.at[hp] for ref in in_refs[i * nb:(i + 1) * nb]] for i in range(5))
    n_stat = len(stat_refs) // (nb - 1)
    n = ATTN_BLOCK
    halo_slab = jnp.where(pl.program_id(2) > 0, 0, 1)

    def block_stats(bi, j, r):
        q_ref, k_ref, kh_ref, v_ref, vh_ref = (
            q_refs[bi], k_refs[bi], kh_refs[bi], v_refs[bi], vh_refs[bi])
        cols = slice(r * LANES, (r + 1) * LANES)
        cur = slice(j * n, (j + 1) * n)
        if j == 0:
            kp, vp, bias = kh_ref[:, cols], vh_ref[:, cols], bias_ref[halo_slab]
        else:
            prev = slice((j - 1) * n, j * n)
            kp, vp, bias = k_ref[prev, cols], v_ref[prev, cols], bias_ref[0]
        return _attn_unit(q_ref[cur, cols], kp, k_ref[cur, cols], vp, v_ref[cur, cols],
                          bias, row_max)

    for bi in range(nb - 1, 0, -1):
        d = DILATIONS[bi]
        for r in range(d):
            for j in range(ATTN_TILE // (d * n)):
                rows = pl.ds(j * n * d + r, n, stride=d)
                for ref, val in zip(stat_refs[n_stat * (bi - 1):n_stat * bi],
                                    block_stats(bi, j, r)):
                    ref[rows, :] = val

    for j in range(ATTN_TILE // n):
        rows = slice(j * n, (j + 1) * n)
        stats = [block_stats(0, j, 0)] + [
            [ref[rows, :] for ref in stat_refs[n_stat * (bi - 1):n_stat * bi]]
            for bi in range(1, nb)]
        if row_max:
            m = functools.reduce(jnp.maximum, [st[0] for st in stats])
            ws = [jnp.exp2(st[0] - m) for st in stats]
            den = sum(w * st[1] for w, st in zip(ws, stats))
            num = sum(w * st[2] for w, st in zip(ws, stats))
        else:
            den = sum(st[0] for st in stats)
            num = sum(st[1] for st in stats)
        y_ref[rows, hp * LANES:(hp + 1) * LANES] = (num / den).astype(y_ref.dtype)


def _dilated_attention(qs, ks, vs, row_max):
    b, hp, s, _ = qs[0].shape
    n = ATTN_BLOCK

    def tile_spec(d):
        return pl.BlockSpec((None, ATTN_HEAD_PAIRS, ATTN_TILE // d, d * LANES),
                            lambda i, h, t: (i, h, t, 0))

    def halo_spec(d):
        per_tile = ATTN_TILE // (d * n)
        return pl.BlockSpec((None, ATTN_HEAD_PAIRS, n, d * LANES),
                            lambda i, h, t: (i, h, jnp.maximum(t * per_tile - 1, 0), 0))

    tiles = [tile_spec(d) for d in DILATIONS]
    halos = [halo_spec(d) for d in DILATIONS]
    args = list(qs) + list(ks) * 2 + list(vs) * 2 + [_band_bias()]
    return pl.pallas_call(
        functools.partial(_attn_kernel, row_max=row_max),
        grid=(b, hp // ATTN_HEAD_PAIRS, s // ATTN_TILE),
        in_specs=tiles + tiles + halos + tiles + halos + [_resident((2, n, 2 * n))],
        out_specs=pl.BlockSpec((None, ATTN_TILE, ATTN_HEAD_PAIRS * LANES),
                               lambda i, h, t: (i, t, h)),
        out_shape=jax.ShapeDtypeStruct((b, s, hp * LANES), _BF16),
        scratch_shapes=([pltpu.VMEM((ATTN_TILE, LANES), _F32)]
                        * (ATTN_HEAD_PAIRS * (3 if row_max else 2) * (len(DILATIONS) - 1))),
        compiler_params=_params(3),
        name="dilated_attention",
    )(*args)


def kernel(x, c, w_ada, b_ada, norm_g, w_in, q_norm_g, k_norm_g, conv_w, conv_b,
           w_out, ffn_w1, ffn_w2):
    b, s, d = x.shape
    depth = w_ada.shape[0]
    a = conv_w.shape[-1]
    n_heads = a // HEAD_DIM
    assert d - a == a and a % (ATTN_HEAD_PAIRS * LANES) == 0
    assert s % ATTN_TILE == 0 and s % ROW_TILE == 0 and DILATIONS[0] == 1
    assert ROW_TILE % (DILATIONS[-1] * 2 * SUBLANES) == 0
    assert q_norm_g.shape[-1] == HEAD_DIM and conv_w.shape[1] == CONV_K
    assert w_ada.shape[-1] == N_SUB * N_MOD * d

    rows = -(-b // SUBLANES) * SUBLANES
    c_pad = jnp.pad(c, ((0, rows - b), (0, 0)))
    mod = _modulation(c_pad, w_ada, b_ada)[:, :b].reshape(depth, b, N_SUB, N_MOD, 1, d)

    w1 = ffn_w1.astype(_BF16)
    w2 = ffn_w2.astype(_BF16)
    w_in_b = w_in.astype(_BF16)
    w_out_b = w_out.astype(_BF16)

    for layer in range(depth):
        shift = lambda i: mod[layer, :, i, 0]
        scale = lambda i: mod[layer, :, i, 1]
        gate = lambda i: mod[layer, :, i, 2]
        g = lambda i: norm_g[layer, i][None, :]

        x = _ffn(x, g(0), shift(0), scale(0), gate(0), w1, w2, (layer, 0))
        qs, ks, vs, yc = _mixer_proj(
            x, g(1), shift(1), scale(1), w_in_b, (layer,),
            jnp.tile(q_norm_g[layer], n_heads)[None, :],
            jnp.tile(k_norm_g[layer], n_heads)[None, :],
            conv_w[layer], conv_b[layer][None, :])
        score_bound = (HEAD_DIM ** 0.5 * LOG2_E * jnp.max(jnp.abs(q_norm_g[layer]))
                       * jnp.max(jnp.abs(k_norm_g[layer])))
        ya = lax.cond(score_bound <= MAX_UNSHIFTED_SCORE,
                      functools.partial(_dilated_attention, row_max=False),
                      functools.partial(_dilated_attention, row_max=True),
                      qs, ks, vs)
        x = _ffn(x, g(2), shift(2), scale(2), gate(2), w1, w2, (layer, 1),
                 mixer=(ya, yc, gate(1), w_out_b, (layer,)))
    return x
```
